```python
import math, functools
import jax, jax.numpy as jnp
from jax import lax
import numpy as np

D_MODEL = 1024
BATCH = 4
SEQ = 4096
DEPTH = 1
DEC_BATCH = 128
DEC_SEQ = 8
PAST_LEN = 8192
PAGE_SIZE = 128

N_HEADS = 8
HEAD_DIM = 64
V_DIM = 2 * HEAD_DIM
D_QK = 2 * N_HEADS * HEAD_DIM
D_ATTN = N_HEADS * V_DIM
D_RNN = D_MODEL
LRU_BLOCKS = 4
LRU_BLOCK_W = D_RNN // LRU_BLOCKS
CONV_WIDTH = 4
LRU_C = 8.0
D_FF = 4 * D_MODEL
NUM_BUCKETS = 32
MAX_DISTANCE = 128
Q_BLOCK = 128
EPS = 1e-6
NEG_INF = -1e30
IN_WIDTH = 2 * D_RNN + 2 * D_QK + D_ATTN + 2 * D_MODEL
IN_CUTS = (D_RNN, 2 * D_RNN, 2 * D_RNN + D_QK, 2 * D_RNN + 2 * D_QK,
           2 * D_RNN + 2 * D_QK + D_ATTN, 2 * D_RNN + 2 * D_QK + D_ATTN + D_MODEL)

kernel_name = 'griffin_diffattn_hybrid_step'


def _rmsnorm(x, g):
    xf = x.astype(jnp.float32)
    y = xf * lax.rsqrt(jnp.mean(xf * xf, axis=-1, keepdims=True) + EPS)
    return (y * g.astype(jnp.float32)).astype(x.dtype)


def _rel_bucket(rel):
    n = jnp.maximum(rel, 0)
    max_exact = NUM_BUCKETS // 2
    large = max_exact + (jnp.log(jnp.maximum(n, max_exact).astype(jnp.float32) / max_exact)
                         / math.log(MAX_DISTANCE / max_exact) * (NUM_BUCKETS - max_exact)).astype(jnp.int32)
    large = jnp.minimum(large, NUM_BUCKETS - 1)
    return jnp.where(n < max_exact, n, large)


def _diff_attention(q, k, v, q_pos, k_pos, rel_table, lam, lam_init, subln_g):
    b, tq = q.shape[:2]
    tk = k.shape[1]
    logits = jnp.einsum('bqhd,bkhd->bhqk', q, k).astype(jnp.float32) * (HEAD_DIM ** -0.5)
    logits = logits.reshape(b, N_HEADS, 2, tq, tk)
    rel = q_pos[:, None] - k_pos[None, :]
    bias = jnp.transpose(rel_table.astype(jnp.float32)[_rel_bucket(rel)], (2, 0, 1))
    logits = jnp.where((rel >= 0)[None, None, None], logits + bias[None, :, None], NEG_INF)
    p = jax.nn.softmax(logits, axis=-1)
    attn = p[:, :, 0] - lam * p[:, :, 1]
    o = jnp.einsum('bhqk,bkhe->bqhe', attn, v.astype(jnp.float32))
    o = _rmsnorm(o, subln_g) * (1.0 - lam_init)
    return o.reshape(b, tq, D_ATTN).astype(v.dtype)


def _attend_prompt(q, k, v, lam, lam_init, subln_g, *, rel_table):
    b, s = q.shape[:2]
    n_blk = s // Q_BLOCK
    q_blocks = q.reshape(b, n_blk, Q_BLOCK, 2 * N_HEADS, HEAD_DIM).transpose(1, 0, 2, 3, 4)
    k_pos = jnp.arange(s)

    def block(args):
        qb, i = args
        q_pos = i * Q_BLOCK + jnp.arange(Q_BLOCK)
        return _diff_attention(qb, k, v, q_pos, k_pos, rel_table, lam, lam_init, subln_g)

    out = lax.map(block, (q_blocks, jnp.arange(n_blk)))
    return out.transpose(1, 0, 2, 3).reshape(b, s, D_ATTN)


def _attend_sample(q, k, v, lam, lam_init, subln_g, *, cache_k, cache_v, page_table, layer, rel_table):
    t = q.shape[1]
    past = page_table.shape[1] * PAGE_SIZE
    k_pos = jnp.arange(past + t)
    q_pos = past + jnp.arange(t)

    def seq(args):
        qs, ks, vs, pages = args
        k_past = cache_k[layer, pages].reshape(past, 2 * N_HEADS, HEAD_DIM).astype(ks.dtype)
        v_past = cache_v[layer, pages].reshape(past, N_HEADS, V_DIM).astype(vs.dtype)
        k_all = jnp.concatenate([k_past, ks], axis=0)[None]
        v_all = jnp.concatenate([v_past, vs], axis=0)[None]
        return _diff_attention(qs[None], k_all, v_all, q_pos, k_pos, rel_table, lam, lam_init, subln_g)[0]

    return lax.map(seq, (q, k, v, page_table))


def _causal_conv(x, prev, w, b):
    t = x.shape[1]
    xpad = jnp.concatenate([prev.astype(x.dtype), x], axis=1)
    y = b + sum(xpad[:, j:j + t] * w[j] for j in range(CONV_WIDTH))
    return y, xpad[:, -(CONV_WIDTH - 1):]


def _rg_lru(x, pos, h_prev, wa, ba, wx, bx, lam):
    b, t, _ = x.shape
    xf = x.astype(jnp.float32)
    xb = xf.reshape(b, t, LRU_BLOCKS, LRU_BLOCK_W)
    gate_r = jax.nn.sigmoid(jnp.einsum('btnc,ncd->btnd', xb, wa.astype(jnp.float32)).reshape(b, t, D_RNN) + ba)
    gate_i = jax.nn.sigmoid(jnp.einsum('btnc,ncd->btnd', xb, wx.astype(jnp.float32)).reshape(b, t, D_RNN) + bx)
    log_a = -LRU_C * gate_r * jax.nn.softplus(-lam.astype(jnp.float32))
    a = jnp.exp(log_a)
    mult = jnp.where((pos == 0)[None, :, None], 1.0, jnp.sqrt(-jnp.expm1(2.0 * log_a)))
    u = mult * gate_i * xf

    def step(h, au):
        a_t, u_t = au
        h = a_t * h + u_t
        return h, h

    h_last, hs = lax.scan(step, h_prev.astype(jnp.float32), (a.transpose(1, 0, 2), u.transpose(1, 0, 2)))
    return hs.transpose(1, 0, 2).astype(x.dtype), h_last.astype(h_prev.dtype)


def _layer(x, c, pos, conv_prev, h_prev, attend, lam_init, *, norm1_g, norm2_g, w_mod, b_mod, w_in,
           conv_w, conv_b, lru_wa, lru_ba, lru_wx, lru_bx, lru_lambda, lambda_q1, lambda_k1,
           lambda_q2, lambda_k2, subln_g, w_out, w_up, w_down):
    b, t, _ = x.shape
    mod = jnp.einsum('bd,de->be', jax.nn.silu(c), w_mod) + b_mod
    shift1, scale1, gate1, shift2, scale2, gate2 = jnp.split(mod[:, None, :], 6, axis=-1)
    h = _rmsnorm(x, norm1_g) * (1.0 + scale1) + shift1
    z = jnp.einsum('btd,de->bte', h, w_in)
    x_rec, y_rec, q, k, v, g_rec, g_att = jnp.split(z, IN_CUTS, axis=-1)
    x_conv, conv_new = _causal_conv(x_rec, conv_prev, conv_w, conv_b)
    h_rec, h_new = _rg_lru(x_conv, pos, h_prev, lru_wa, lru_ba, lru_wx, lru_bx, lru_lambda)
    o_rec = h_rec * jax.nn.gelu(y_rec)
    q = q.reshape(b, t, 2 * N_HEADS, HEAD_DIM)
    k = k.reshape(b, t, 2 * N_HEADS, HEAD_DIM)
    v = v.reshape(b, t, N_HEADS, V_DIM)
    lam = (jnp.exp(jnp.sum(lambda_q1.astype(jnp.float32) * lambda_k1))
           - jnp.exp(jnp.sum(lambda_q2.astype(jnp.float32) * lambda_k2)) + lam_init)
    o_att = attend(q, k, v, lam, lam_init, subln_g)
    merged = jax.nn.sigmoid(g_rec) * o_rec + jax.nn.sigmoid(g_att) * o_att
    x = x + gate1 * jnp.einsum('bte,ed->btd', merged, w_out)
    h2 = _rmsnorm(x, norm2_g) * (1.0 + scale2) + shift2
    ff = jnp.square(jax.nn.relu(jnp.einsum('btd,df->btf', h2, w_up)))
    x = x + gate2 * jnp.einsum('btf,fd->btd', ff, w_down)
    return x, k, v, conv_new, h_new


def setup_inputs(seed: int = 0) -> dict:
    key = jax.random.key(seed)
    ks = jax.random.split(key, 32)
    f32 = jnp.float32

    def nrm(k, shape, scale=1.0):
        return scale * jax.random.normal(k, shape, f32)

    n_pages = PAST_LEN // PAGE_SIZE
    n_used = DEC_BATCH * n_pages
    n_pool = n_used + n_used // 4
    page_table = jax.random.permutation(ks[0], n_pool)[:n_used].reshape(DEC_BATCH, n_pages).astype(jnp.int32)
    u = jax.random.uniform(ks[1], (DEPTH, D_RNN), f32, minval=0.9, maxval=0.999)
    a0 = u ** (1.0 / LRU_C)
    return {
        'x_prompt': nrm(ks[2], (BATCH, SEQ, D_MODEL)),
        'x_sample': nrm(ks[3], (DEC_BATCH, DEC_SEQ, D_MODEL)),
        'cache_k': nrm(ks[4], (DEPTH, n_pool, PAGE_SIZE, 2 * N_HEADS, HEAD_DIM)),
        'cache_v': nrm(ks[5], (DEPTH, n_pool, PAGE_SIZE, N_HEADS, V_DIM)),
        'state_conv': nrm(ks[6], (DEPTH, DEC_BATCH, CONV_WIDTH - 1, D_RNN)),
        'state_lru': nrm(ks[7], (DEPTH, DEC_BATCH, D_RNN), 0.5),
        'page_table': page_table,
        'c_prompt': nrm(ks[8], (BATCH, D_MODEL)),
        'c_sample': nrm(ks[9], (DEC_BATCH, D_MODEL)),
        'rel_bias_table': nrm(ks[10], (NUM_BUCKETS, N_HEADS), 0.5),
        'norm1_g': 1.0 + nrm(ks[11], (DEPTH, D_MODEL), 0.05),
        'norm2_g': 1.0 + nrm(ks[12], (DEPTH, D_MODEL), 0.05),
        'w_mod': nrm(ks[13], (DEPTH, D_MODEL, 6 * D_MODEL), D_MODEL ** -0.5),
        'b_mod': nrm(ks[14], (DEPTH, 6 * D_MODEL), 0.01),
        'w_in': nrm(ks[15], (DEPTH, D_MODEL, IN_WIDTH), D_MODEL ** -0.5),
        'conv_w': nrm(ks[16], (DEPTH, CONV_WIDTH, D_RNN), CONV_WIDTH ** -0.5),
        'conv_b': nrm(ks[17], (DEPTH, D_RNN), 0.01),
        'lru_wa': nrm(ks[18], (DEPTH, LRU_BLOCKS, LRU_BLOCK_W, LRU_BLOCK_W), LRU_BLOCK_W ** -0.5),
        'lru_ba': nrm(ks[19], (DEPTH, D_RNN), 0.01),
        'lru_wx': nrm(ks[20], (DEPTH, LRU_BLOCKS, LRU_BLOCK_W, LRU_BLOCK_W), LRU_BLOCK_W ** -0.5),
        'lru_bx': nrm(ks[21], (DEPTH, D_RNN), 0.01),
        'lru_lambda': jnp.log(a0) - jnp.log1p(-a0),
        'lambda_q1': nrm(ks[22], (DEPTH, HEAD_DIM), 0.1),
        'lambda_k1': nrm(ks[23], (DEPTH, HEAD_DIM), 0.1),
        'lambda_q2': nrm(ks[24], (DEPTH, HEAD_DIM), 0.1),
        'lambda_k2': nrm(ks[25], (DEPTH, HEAD_DIM), 0.1),
        'subln_g': 1.0 + nrm(ks[26], (DEPTH, V_DIM), 0.05),
        'w_out': nrm(ks[27], (DEPTH, D_MODEL, D_MODEL), D_MODEL ** -0.5),
        'w_up': nrm(ks[28], (DEPTH, D_MODEL, D_FF), D_MODEL ** -0.5),
        'w_down': nrm(ks[29], (DEPTH, D_FF, D_MODEL), D_FF ** -0.5),
        'final_norm_g': 1.0 + nrm(ks[30], (D_MODEL,), 0.05),
    }


def reference(x_prompt, x_sample, cache_k, cache_v, state_conv, state_lru, page_table, c_prompt, c_sample,
              rel_bias_table, norm1_g, norm2_g, w_mod, b_mod, w_in, conv_w, conv_b, lru_wa, lru_ba, lru_wx,
              lru_bx, lru_lambda, lambda_q1, lambda_k1, lambda_q2, lambda_k2, subln_g, w_out, w_up, w_down,
              final_norm_g):
    b_p, s = x_prompt.shape[:2]
    t = x_sample.shape[1]
    pos_prompt = jnp.arange(s)
    pos_sample = PAST_LEN + jnp.arange(t)
    conv0 = jnp.zeros((b_p, CONV_WIDTH - 1, D_RNN), x_prompt.dtype)
    h0 = jnp.zeros((b_p, D_RNN), x_prompt.dtype)
    hp, hs = x_prompt, x_sample
    kp_l, vp_l, cp_l, lp_l, ks_l, vs_l, cs_l, ls_l = [], [], [], [], [], [], [], []
    attend_prompt = functools.partial(_attend_prompt, rel_table=rel_bias_table)
    for l in range(DEPTH):
        lam_init = 0.8 - 0.6 * math.exp(-0.3 * l)
        lw = dict(norm1_g=norm1_g[l], norm2_g=norm2_g[l], w_mod=w_mod[l], b_mod=b_mod[l], w_in=w_in[l],
                  conv_w=conv_w[l], conv_b=conv_b[l], lru_wa=lru_wa[l], lru_ba=lru_ba[l], lru_wx=lru_wx[l],
                  lru_bx=lru_bx[l], lru_lambda=lru_lambda[l], lambda_q1=lambda_q1[l], lambda_k1=lambda_k1[l],
                  lambda_q2=lambda_q2[l], lambda_k2=lambda_k2[l], subln_g=subln_g[l], w_out=w_out[l],
                  w_up=w_up[l], w_down=w_down[l])
        attend_sample = functools.partial(_attend_sample, cache_k=cache_k, cache_v=cache_v,
                                          page_table=page_table, layer=l, rel_table=rel_bias_table)
        hp, kp, vp, cp, lp = _layer(hp, c_prompt, pos_prompt, conv0, h0, attend_prompt, lam_init, **lw)
        hs, k_s, v_s, c_s, l_s = _layer(hs, c_sample, pos_sample, state_conv[l], state_lru[l],
                                        attend_sample, lam_init, **lw)
        kp_l.append(kp); vp_l.append(vp); cp_l.append(cp); lp_l.append(lp)
        ks_l.append(k_s); vs_l.append(v_s); cs_l.append(c_s); ls_l.append(l_s)
    y_prompt = _rmsnorm(hp, final_norm_g)
    y_sample = _rmsnorm(hs, final_norm_g)
    return (y_prompt, y_sample,
            jnp.stack(kp_l), jnp.stack(vp_l), jnp.stack(cp_l), jnp.stack(lp_l),
            jnp.stack(ks_l), jnp.stack(vs_l), jnp.stack(cs_l), jnp.stack(ls_l))
```

```python
import functools
import math

import numpy as np
import jax
import jax.numpy as jnp
from jax import lax
from jax.experimental import pallas as pl
from jax.experimental.pallas import tpu as pltpu

F32 = jnp.float32
BF16 = jnp.bfloat16

N_HEADS = 8
HEAD_DIM = 64
V_DIM = 2 * HEAD_DIM
LRU_BLOCKS = 4
CONV_WIDTH = 4
LRU_C = 8.0
NUM_BUCKETS = 32
MAX_DISTANCE = 128
EPS = 1e-6
NEG_INF = -1e30
N_SEG = 7
K_SEG = 3
LANES = 128
SUBLANES = 8
VMEM_LIMIT = 56 * 1024 * 1024


def _cparams(sem):
    return pltpu.CompilerParams(dimension_semantics=sem, vmem_limit_bytes=VMEM_LIMIT)


def _mod_kernel(c_ref, w_ref, b_ref, o_ref):
    c = c_ref[...]
    s = (c * jax.nn.sigmoid(c)).astype(BF16)
    o_ref[...] = jnp.dot(s, w_ref[...].astype(BF16), preferred_element_type=F32) + b_ref[...]


def _modulation(c_all, w_mod, b_mod):
    m, d = c_all.shape
    n = w_mod.shape[1]
    tn = d
    return pl.pallas_call(
        _mod_kernel,
        grid=(n // tn,),
        in_specs=[pl.BlockSpec((m, d), lambda j: (0, 0)),
                  pl.BlockSpec((d, tn), lambda j: (0, j)),
                  pl.BlockSpec((1, tn), lambda j: (0, j))],
        out_specs=pl.BlockSpec((m, tn), lambda j: (0, j)),
        out_shape=jax.ShapeDtypeStruct((m, n), F32),
        compiler_params=_cparams(("arbitrary",)),
        name="modulation",
    )(c_all, w_mod, b_mod.reshape(1, n))


def _inproj_kernel(x_ref, sc_ref, sh_ref, g_ref, w_ref, *rest, keys_transposed):
    if keys_transposed:
        xrec_ref, yrec_ref, q_ref, kf_ref, kb_ref, vf_ref, vb_ref, grec_ref, gatt_ref, h_scr = rest
    else:
        xrec_ref, yrec_ref, q_ref, kf_ref, vf_ref, grec_ref, gatt_ref, h_scr = rest
    j = pl.program_id(1)

    @pl.when(j == 0)
    def _():
        x = x_ref[...]
        ms = jnp.mean(x * x, axis=-1, keepdims=True)
        y = x * lax.rsqrt(ms + EPS) * g_ref[...]
        h_scr[...] = (y * (1.0 + sc_ref[0]) + sh_ref[0]).astype(BF16)

    def project():
        return jnp.dot(h_scr[...], w_ref[...], preferred_element_type=F32)

    @pl.when(j == 0)
    def _():
        xrec_ref[...] = project()

    @pl.when(j == 1)
    def _():
        yrec_ref[...] = project()

    @pl.when(j == 2)
    def _():
        q_ref[...] = (project() * (HEAD_DIM ** -0.5)).astype(BF16)

    @pl.when(j == K_SEG)
    def _():
        if keys_transposed:
            zt = lax.dot_general(w_ref[...], h_scr[...], (((1,), (1,)), ((), ())), preferred_element_type=F32)
            kf_ref[0] = zt
            kb_ref[0] = zt.astype(BF16)
        else:
            kf_ref[...] = project()

    @pl.when(j == 4)
    def _():
        z = project()
        vf_ref[...] = z
        if keys_transposed:
            vb_ref[...] = z.astype(BF16)

    @pl.when(j == 5)
    def _():
        grec_ref[...] = project()

    @pl.when(j == 6)
    def _():
        gatt_ref[...] = project()


def _in_projection(x2, scale, shift, norm_g, w_seg, tm, mod_map, seq_len, keys_transposed):
    n, d = x2.shape
    mod_block = (1,) + scale.shape[1:]
    tok = lambda i, j: (i, 0)
    f32_out = jax.ShapeDtypeStruct((n, d), F32)
    bf_out = jax.ShapeDtypeStruct((n, d), BF16)
    out_spec = pl.BlockSpec((tm, d), tok)
    if keys_transposed:
        tps = seq_len // tm
        kt_spec = pl.BlockSpec((1, d, tm), lambda i, j: (i // tps, 0, i % tps))
        kt_f32 = jax.ShapeDtypeStruct((n // seq_len, d, seq_len), F32)
        kt_bf = jax.ShapeDtypeStruct((n // seq_len, d, seq_len), BF16)
        out_specs = [out_spec, out_spec, out_spec, kt_spec, kt_spec, out_spec, out_spec, out_spec, out_spec]
        out_shape = [f32_out, f32_out, bf_out, kt_f32, kt_bf, f32_out, bf_out, f32_out, f32_out]
    else:
        out_specs = [out_spec] * 7
        out_shape = [f32_out, f32_out, bf_out, f32_out, f32_out, f32_out, f32_out]
    return pl.pallas_call(
        functools.partial(_inproj_kernel, keys_transposed=keys_transposed),
        grid=(n // tm, N_SEG),
        in_specs=[pl.BlockSpec((tm, d), tok),
                  pl.BlockSpec(mod_block, mod_map),
                  pl.BlockSpec(mod_block, mod_map),
                  pl.BlockSpec((1, d), lambda i, j: (0, 0)),
                  pl.BlockSpec((None, d, d), lambda i, j: (j, 0, 0))],
        out_specs=out_specs,
        out_shape=out_shape,
        scratch_shapes=[pltpu.VMEM((tm, d), BF16)],
        compiler_params=_cparams(("arbitrary", "arbitrary")),
        name="in_projection",
    )(x2, scale, shift, norm_g.reshape(1, d), w_seg)


def _rec_kernel(x_ref, y_ref, g_ref, cprev_ref, hprev_ref, cw_ref, cb_ref, wa_ref, ba_ref, wx_ref, bx_ref,
                lam_ref, o_ref, cnew_ref, hnew_ref, xpad, a_scr, u_scr, hcar, *, nb, t, pos0):
    d = x_ref.shape[-1]
    ti = pl.program_id(1)
    groups = t // SUBLANES
    r = nb * groups

    @pl.when(ti == 0)
    def _():
        xpad[:, SUBLANES - (CONV_WIDTH - 1):SUBLANES, :] = cprev_ref[...]
        hcar[...] = hprev_ref[...]

    @pl.when(ti > 0)
    def _():
        xpad[:, 0:SUBLANES, :] = xpad[:, t:t + SUBLANES, :]

    x = x_ref[...]
    xpad[:, SUBLANES:, :] = x
    xc = cb_ref[...] + cw_ref[CONV_WIDTH - 1:CONV_WIDTH, :] * x
    for jj in range(CONV_WIDTH - 1):
        off = SUBLANES - (CONV_WIDTH - 1) + jj
        xc = xc + cw_ref[jj:jj + 1, :] * xpad[:, off:off + t, :]
    cnew_ref[...] = xpad[:, t + SUBLANES - (CONV_WIDTH - 1):t + SUBLANES, :]

    xc2 = xc.reshape(nb * t, d)
    xcb = xc2.astype(BF16)
    bw = d // LRU_BLOCKS
    ga = jnp.concatenate([jnp.dot(xcb[:, n * bw:(n + 1) * bw], wa_ref[n], preferred_element_type=F32)
                          for n in range(LRU_BLOCKS)], axis=-1)
    gx = jnp.concatenate([jnp.dot(xcb[:, n * bw:(n + 1) * bw], wx_ref[n], preferred_element_type=F32)
                          for n in range(LRU_BLOCKS)], axis=-1)
    gate_r = jax.nn.sigmoid(ga + ba_ref[...])
    gate_i = jax.nn.sigmoid(gx + bx_ref[...])
    z = -lam_ref[...]
    softplus = jnp.maximum(z, 0.0) + jnp.log1p(jnp.exp(-jnp.abs(z)))
    log_a = (-LRU_C) * gate_r * softplus
    a = jnp.exp(log_a)
    mult = jnp.sqrt(jnp.tanh(-log_a) * (1.0 + a * a))
    pos = pos0 + ti * t + lax.broadcasted_iota(jnp.int32, (nb, t, d), 1).reshape(nb * t, d)
    mult = jnp.where(pos == 0, 1.0, mult)
    u = mult * gate_i * xc2

    a3 = a.reshape(r, SUBLANES, d)
    u3 = u.reshape(r, SUBLANES, d)
    row = lax.broadcasted_iota(jnp.int32, (r, SUBLANES, d), 1)
    for s in (1, 2, 4):
        a_sh = pltpu.roll(a3, s, axis=1)
        u_sh = pltpu.roll(u3, s, axis=1)
        ok = row >= s
        u3 = jnp.where(ok, a3 * u_sh + u3, u3)
        a3 = jnp.where(ok, a3 * a_sh, a3)

    if groups == 1:
        h3 = a3 * hcar[...] + u3
        hcar[...] = h3[:, SUBLANES - 1:SUBLANES, :]
        h2 = h3.reshape(nb * t, d)
    else:
        assert nb == 1
        a_scr[...] = a3
        u_scr[...] = u3

        def body(gi, hin):
            hg = a_scr[gi] * hin + u_scr[gi]
            u_scr[gi] = hg
            return hg[SUBLANES - 1:SUBLANES, :]

        hcar[0] = lax.fori_loop(0, groups, body, hcar[0])
        h2 = u_scr[...].reshape(nb * t, d)
    hnew_ref[...] = hcar[...]
    yv = y_ref[...].reshape(nb * t, d)
    gv = g_ref[...].reshape(nb * t, d)
    o_ref[...] = (jax.nn.sigmoid(gv) * (h2 * jax.nn.gelu(yv))).reshape(nb, t, d)


def _recurrent_branch(xrec, yrec, grec, conv_prev, h_prev, conv_w, conv_b, wa_bf, ba, wx_bf, bx, lam, nb, t, pos0):
    b, s, d = xrec.shape
    groups = t // SUBLANES
    r = nb * groups
    blk = pl.BlockSpec((nb, t, d), lambda bi, ti: (bi, ti, 0))
    vec = pl.BlockSpec((1, d), lambda bi, ti: (0, 0))
    wblk = pl.BlockSpec(wa_bf.shape, lambda bi, ti: (0, 0, 0))
    kern = functools.partial(_rec_kernel, nb=nb, t=t, pos0=pos0)
    return pl.pallas_call(
        kern,
        grid=(b // nb, s // t),
        in_specs=[blk, blk, blk,
                  pl.BlockSpec((nb, CONV_WIDTH - 1, d), lambda bi, ti: (bi, 0, 0)),
                  pl.BlockSpec((nb, 1, d), lambda bi, ti: (bi, 0, 0)),
                  pl.BlockSpec((CONV_WIDTH, d), lambda bi, ti: (0, 0)),
                  vec, wblk, vec, wblk, vec, vec],
        out_specs=[blk,
                   pl.BlockSpec((nb, CONV_WIDTH - 1, d), lambda bi, ti: (bi, 0, 0)),
                   pl.BlockSpec((nb, 1, d), lambda bi, ti: (bi, 0, 0))],
        out_shape=[jax.ShapeDtypeStruct((b, s, d), F32),
                   jax.ShapeDtypeStruct((b, CONV_WIDTH - 1, d), F32),
                   jax.ShapeDtypeStruct((b, 1, d), F32)],
        scratch_shapes=[pltpu.VMEM((nb, t + SUBLANES, d), F32),
                        pltpu.VMEM((r, SUBLANES, d), F32),
                        pltpu.VMEM((r, SUBLANES, d), F32),
                        pltpu.VMEM((nb, 1, d), F32)],
        compiler_params=_cparams(("arbitrary", "arbitrary")),
        name="recurrent_branch",
    )(xrec, yrec, grec, conv_prev, h_prev.reshape(b, 1, d), conv_w, conv_b.reshape(1, d), wa_bf,
      ba.reshape(1, d), wx_bf, bx.reshape(1, d), lam.reshape(1, d))


def _bucket_np(rel):
    n = np.maximum(rel, 0)
    max_exact = NUM_BUCKETS // 2
    ratio = np.log(np.maximum(n, max_exact).astype(np.float32) / np.float32(max_exact)) / np.float32(
        math.log(MAX_DISTANCE / max_exact))
    large = max_exact + (ratio * np.float32(NUM_BUCKETS - max_exact)).astype(np.int32)
    large = np.minimum(large, NUM_BUCKETS - 1)
    return np.where(n < max_exact, n, large).astype(np.int32)


def _far_distance():
    far = 1
    while not np.all(_bucket_np(np.arange(far, far + 4 * MAX_DISTANCE)) == NUM_BUCKETS - 1):
        far += 1
    return far


def _bias_from_dist(dist, table, far_bias):
    vals = table[jnp.asarray(_bucket_np(dist))] - far_bias
    vals = jnp.where(jnp.asarray(dist >= 0)[..., None], vals, NEG_INF)
    return jnp.transpose(vals, (2, 0, 1))


def _pattn_kernel(lam_ref, q_ref, kt_ref, v_ref, bd_ref, bs_ref, g_ref, o_ref, m_scr, l_scr, acc_scr, *, t, lam_init):
    i = pl.program_id(2)
    q = q_ref[0]
    lane = lax.broadcasted_iota(jnp.int32, q.shape, 1)
    zero = jnp.zeros_like(q)
    qz = (jnp.where(lane < HEAD_DIM, q, zero), jnp.where(lane >= HEAD_DIM, q, zero))
    m_scr[...] = jnp.full(m_scr.shape, NEG_INF, F32)
    l_scr[...] = jnp.zeros(l_scr.shape, F32)
    acc_scr[...] = jnp.zeros(acc_scr.shape, F32)
    rep = t // LANES

    def block(kb, bias):
        start = pl.multiple_of(kb * t, t)
        kt = kt_ref[0, :, pl.ds(start, t)]
        v = v_ref[0, pl.ds(start, t), :]
        for mp in range(2):
            s = jnp.dot(qz[mp], kt, preferred_element_type=F32)
            if bias is not None:
                s = s + bias
            m_old = m_scr[mp]
            m_new = jnp.maximum(m_old, jnp.max(s, axis=-1, keepdims=True))
            p = jnp.exp(s - pltpu.repeat(m_new, rep, axis=1))
            alpha = jnp.exp(m_old - m_new)
            l_scr[mp] = alpha * l_scr[mp] + jnp.sum(p, axis=-1, keepdims=True)
            acc_scr[mp] = alpha * acc_scr[mp] + jnp.dot(p.astype(BF16), v, preferred_element_type=F32)
            m_scr[mp] = m_new

    def far(kb, carry):
        block(kb, None)
        return carry

    lax.fori_loop(0, jnp.maximum(i - 1, 0), far, 0)

    @pl.when(i >= 1)
    def _():
        block(i - 1, bs_ref[0])

    block(i, bd_ref[0])

    lam = lam_ref[0]
    o = acc_scr[0] / l_scr[0] - lam * (acc_scr[1] / l_scr[1])
    ms = jnp.mean(o * o, axis=-1, keepdims=True)
    o_ref[0] = o * lax.rsqrt(ms + EPS) * g_ref[...] * (1.0 - lam_init)


def _prompt_attention(q, kt, v, rel_table, lam, lam_init, subln_g, t):
    b, s, d = q.shape
    far = _far_distance()
    assert t + 1 >= far, "blocks two or more below the diagonal must see a constant bias"
    far_bias = rel_table[NUM_BUCKETS - 1]
    rr = np.arange(t)[:, None]
    cc = np.arange(t)[None, :]
    bias_diag = _bias_from_dist(rr - cc, rel_table, far_bias)
    bias_sub = _bias_from_dist(t + rr - cc, rel_table, far_bias)
    kern = functools.partial(_pattn_kernel, t=t, lam_init=lam_init)
    return pl.pallas_call(
        kern,
        grid=(b, N_HEADS, s // t),
        in_specs=[pl.BlockSpec(memory_space=pltpu.SMEM),
                  pl.BlockSpec((1, t, V_DIM), lambda bi, h, i: (bi, i, h)),
                  pl.BlockSpec((1, V_DIM, s), lambda bi, h, i: (bi, h, 0)),
                  pl.BlockSpec((1, s, V_DIM), lambda bi, h, i: (bi, 0, h)),
                  pl.BlockSpec((1, t, t), lambda bi, h, i: (h, 0, 0)),
                  pl.BlockSpec((1, t, t), lambda bi, h, i: (h, 0, 0)),
                  pl.BlockSpec((1, V_DIM), lambda bi, h, i: (0, 0))],
        out_specs=pl.BlockSpec((1, t, V_DIM), lambda bi, h, i: (bi, i, h)),
        out_shape=jax.ShapeDtypeStruct((b, s, d), F32),
        scratch_shapes=[pltpu.VMEM((2, t, LANES), F32),
                        pltpu.VMEM((2, t, LANES), F32),
                        pltpu.VMEM((2, t, V_DIM), F32)],
        compiler_params=_cparams(("arbitrary", "arbitrary", "arbitrary")),
        name="prompt_attention",
    )(lam.reshape(1), q, kt, v, bias_diag, bias_sub, subln_g.reshape(1, V_DIM))


def _sattn_kernel(pt_ref, lam_ref, q_ref, knew_ref, vnew_ref, blast_ref, bnew_ref, g_ref, *rest, gp, lam_init):
    k_refs = rest[:gp]
    v_refs = rest[gp:2 * gp]
    o_ref = rest[2 * gp]
    qbd, knew_t, vpad, m_scr, l_scr, acc = rest[2 * gp + 1:]
    del pt_ref
    gi = pl.program_id(1)
    n_maps = 2 * N_HEADS
    t = q_ref.shape[1]
    d = q_ref.shape[2]
    rows = n_maps * t
    page = k_refs[0].shape[1]

    @pl.when(gi == 0)
    def _():
        qf = q_ref[0].astype(F32)
        qt = jnp.concatenate([qf] * n_maps, axis=0)
        rmap = lax.broadcasted_iota(jnp.int32, (rows, d), 0) // t
        cmap = lax.broadcasted_iota(jnp.int32, (rows, d), 1) // HEAD_DIM
        qbd[...] = jnp.where(rmap == cmap, qt, 0.0).astype(BF16)
        pad = jnp.zeros((page - t, d), F32)
        knew_t[...] = jnp.concatenate([knew_ref[0], pad], axis=0).T.astype(BF16)
        vpad[...] = jnp.concatenate([vnew_ref[0], pad], axis=0).astype(BF16)
        m_scr[...] = jnp.full(m_scr.shape, NEG_INF, F32)
        l_scr[...] = jnp.zeros(l_scr.shape, F32)
        acc[...] = jnp.zeros(acc.shape, F32)

    def update(kt, v_heads, bias):
        n = kt.shape[1]
        s = jnp.dot(qbd[...], kt, preferred_element_type=F32)
        if bias is not None:
            s = s + bias
        m_old = m_scr[...]
        m_new = jnp.maximum(m_old, jnp.max(s, axis=-1, keepdims=True))
        p = jnp.exp(s - pltpu.repeat(m_new, n // LANES, axis=1))
        alpha = jnp.exp(m_old - m_new)
        l_scr[...] = alpha * l_scr[...] + jnp.sum(p, axis=-1, keepdims=True)
        m_scr[...] = m_new
        pb = p.astype(BF16)
        pv = jnp.concatenate([jnp.dot(pb[2 * t * h:2 * t * (h + 1), :], v_heads[h], preferred_element_type=F32)
                              for h in range(N_HEADS)], axis=0)
        acc[...] = alpha * acc[...] + pv

    def pages():
        kt = jnp.concatenate([kr[...].astype(BF16) for kr in k_refs], axis=1)
        v_heads = [jnp.concatenate([vr[pl.ds(h, page, stride=N_HEADS), :].astype(BF16) for vr in v_refs], axis=0)
                   for h in range(N_HEADS)]
        return kt, v_heads

    last = pl.num_programs(1) - 1

    @pl.when(gi < last)
    def _():
        kt, v_heads = pages()
        update(kt, v_heads, None)

    @pl.when(gi == last)
    def _():
        kt, v_heads = pages()
        update(kt, v_heads, blast_ref[...])
        update(knew_t[...], [vpad[:, h * V_DIM:(h + 1) * V_DIM] for h in range(N_HEADS)], bnew_ref[...])
        lam = lam_ref[0]
        on = acc[...] / l_scr[...]
        for h in range(N_HEADS):
            r0 = 2 * h * t
            o = on[r0:r0 + t, :] - lam * on[r0 + t:r0 + 2 * t, :]
            ms = jnp.mean(o * o, axis=-1, keepdims=True)
            o_ref[0, :, h * V_DIM:(h + 1) * V_DIM] = o * lax.rsqrt(ms + EPS) * g_ref[...] * (1.0 - lam_init)


def _sample_attention(q, k_new, v_new, cache_kt, cache_v2, page_table, rel_table, lam, lam_init, subln_g, gp):
    b, t, d = q.shape
    page = cache_kt.shape[2]
    n_pages = page_table.shape[1]
    past = n_pages * page
    rows = 2 * N_HEADS * t
    assert rows == LANES and V_DIM == LANES and page == LANES and n_pages % gp == 0
    far = _far_distance()
    keys = gp * page
    assert past - ((n_pages - gp) * page - 1) >= far, "all page groups but the last must see a constant bias"
    far_bias = rel_table[NUM_BUCKETS - 1]
    qpos = past + np.arange(t)
    key_last = (n_pages - gp) * page + np.arange(keys)
    bl = _bias_from_dist(qpos[:, None] - key_last[None, :], rel_table, far_bias)
    bias_last = jnp.repeat(bl, 2, axis=0).reshape(rows, keys)
    key_new = np.full((page,), 10 * (past + t), np.int64)
    key_new[:t] = past + np.arange(t)
    bn = _bias_from_dist(qpos[:, None] - key_new[None, :], rel_table, far_bias)
    bias_new = jnp.repeat(bn, 2, axis=0).reshape(rows, page)

    def page_map(jj):
        return lambda bi, gi, pt: (pt[bi, gi * gp + jj], 0, 0)

    seq = lambda bi, gi, pt: (bi, 0, 0)
    fixed = lambda bi, gi, pt: (0, 0)
    k_specs = [pl.BlockSpec((None, d, page), page_map(jj)) for jj in range(gp)]
    v_specs = [pl.BlockSpec((None, page * N_HEADS, V_DIM), page_map(jj)) for jj in range(gp)]
    kern = functools.partial(_sattn_kernel, gp=gp, lam_init=lam_init)
    grid_spec = pltpu.PrefetchScalarGridSpec(
        num_scalar_prefetch=1,
        grid=(b, n_pages // gp),
        in_specs=[pl.BlockSpec(memory_space=pltpu.SMEM),
                  pl.BlockSpec((1, t, d), seq),
                  pl.BlockSpec((1, t, d), seq),
                  pl.BlockSpec((1, t, d), seq),
                  pl.BlockSpec((rows, keys), fixed),
                  pl.BlockSpec((rows, page), fixed),
                  pl.BlockSpec((1, V_DIM), fixed)] + k_specs + v_specs,
        out_specs=pl.BlockSpec((1, t, d), seq),
        scratch_shapes=[pltpu.VMEM((rows, d), BF16),
                        pltpu.VMEM((d, page), BF16),
                        pltpu.VMEM((page, d), BF16),
                        pltpu.VMEM((rows, LANES), F32),
                        pltpu.VMEM((rows, LANES), F32),
                        pltpu.VMEM((rows, V_DIM), F32)])
    return pl.pallas_call(
        kern,
        grid_spec=grid_spec,
        out_shape=jax.ShapeDtypeStruct((b, t, d), F32),
        compiler_params=_cparams(("arbitrary", "arbitrary")),
        name="sample_attention",
    )(page_table, lam.reshape(1), q, k_new, v_new, bias_last, bias_new, subln_g.reshape(1, V_DIM),
      *([cache_kt] * gp), *([cache_v2] * gp))


def _outproj_kernel(rec_ref, att_ref, gatt_ref, x_ref, gate_ref, w_ref, o_ref):
    merged = rec_ref[...] + jax.nn.sigmoid(gatt_ref[...]) * att_ref[...]
    y = jnp.dot(merged.astype(BF16), w_ref[...], preferred_element_type=F32)
    o_ref[...] = x_ref[...] + gate_ref[0] * y


def _out_projection(rec, att, gatt, x2, gate, w_out_bf, tm, mod_map):
    n, d = x2.shape
    tok = pl.BlockSpec((tm, d), lambda i: (i, 0))
    mod_block = (1,) + gate.shape[1:]
    return pl.pallas_call(
        _outproj_kernel,
        grid=(n // tm,),
        in_specs=[tok, tok, tok, tok,
                  pl.BlockSpec(mod_block, lambda i: mod_map(i, 0)),
                  pl.BlockSpec((d, d), lambda i: (0, 0))],
        out_specs=tok,
        out_shape=jax.ShapeDtypeStruct((n, d), F32),
        compiler_params=_cparams(("arbitrary",)),
        name="out_projection",
    )(rec, att, gatt, x2, gate, w_out_bf)


def _ffn_kernel(x_ref, sc_ref, sh_ref, gate_ref, g2_ref, gf_ref, wu_ref, wd_ref, o_ref, h_scr, acc_scr):
    f = pl.program_id(1)

    @pl.when(f == 0)
    def _():
        x = x_ref[...]
        ms = jnp.mean(x * x, axis=-1, keepdims=True)
        y = x * lax.rsqrt(ms + EPS) * g2_ref[...]
        h_scr[...] = (y * (1.0 + sc_ref[0]) + sh_ref[0]).astype(BF16)
        acc_scr[...] = jnp.zeros(acc_scr.shape, F32)

    up = jnp.dot(h_scr[...], wu_ref[...], preferred_element_type=F32)
    act = jnp.square(jnp.maximum(up, 0.0)).astype(BF16)
    acc_scr[...] += jnp.dot(act, wd_ref[...], preferred_element_type=F32)

    @pl.when(f == pl.num_programs(1) - 1)
    def _():
        x = x_ref[...] + gate_ref[0] * acc_scr[...]
        ms = jnp.mean(x * x, axis=-1, keepdims=True)
        o_ref[...] = x * lax.rsqrt(ms + EPS) * gf_ref[...]


def _ffn(x2, scale, shift, gate, norm2_g, final_g, w_up_bf, w_down_bf, tm, tf, mod_map):
    n, d = x2.shape
    dff = w_up_bf.shape[1]
    tok = pl.BlockSpec((tm, d), lambda i, f: (i, 0))
    mod_block = (1,) + scale.shape[1:]
    mod_spec = pl.BlockSpec(mod_block, mod_map)
    vec = pl.BlockSpec((1, d), lambda i, f: (0, 0))
    return pl.pallas_call(
        _ffn_kernel,
        grid=(n // tm, dff // tf),
        in_specs=[tok, mod_spec, mod_spec, mod_spec, vec, vec,
                  pl.BlockSpec((d, tf), lambda i, f: (0, f)),
                  pl.BlockSpec((tf, d), lambda i, f: (f, 0))],
        out_specs=tok,
        out_shape=jax.ShapeDtypeStruct((n, d), F32),
        scratch_shapes=[pltpu.VMEM((tm, d), BF16), pltpu.VMEM((tm, d), F32)],
        compiler_params=_cparams(("arbitrary", "arbitrary")),
        name="ffn",
    )(x2, scale, shift, gate, norm2_g.reshape(1, d), final_g.reshape(1, d), w_up_bf, w_down_bf)


def _pick(n, pref):
    t = min(n, pref)
    assert n % t == 0
    return t


def kernel(x_prompt, x_sample, cache_k, cache_v, state_conv, state_lru, page_table, c_prompt, c_sample, rel_bias_table, norm1_g, norm2_g, w_mod, b_mod, w_in, conv_w, conv_b, lru_wa, lru_ba, lru_wx, lru_bx, lru_lambda, lambda_q1, lambda_k1, lambda_q2, lambda_k2, subln_g, w_out, w_up, w_down, final_norm_g):
    depth = w_in.shape[0]
    assert depth == 1
    layer = 0
    lam_init = 0.8 - 0.6 * math.exp(-0.3 * layer)
    bp, s, d = x_prompt.shape
    bs, t, _ = x_sample.shape
    n_pool, page = cache_k.shape[1], cache_k.shape[2]

    n_seq = bp + bs
    n_pad = -n_seq % (2 * SUBLANES)
    c_all = jnp.concatenate([c_prompt, c_sample, jnp.zeros((n_pad, d), F32)], axis=0)
    mod = _modulation(c_all, w_mod[layer], b_mod[layer])
    mods_p = [mod[:bp, i * d:(i + 1) * d].reshape(bp, 1, d) for i in range(6)]

    w_seg = jnp.transpose(w_in[layer].astype(BF16).reshape(d, N_SEG, d), (1, 0, 2))
    w_seg_kt = w_seg.at[K_SEG].set(w_seg[K_SEG].T)
    w_out_bf = w_out[layer].astype(BF16)
    w_up_bf = w_up[layer].astype(BF16)
    w_down_bf = w_down[layer].astype(BF16)
    wa_bf = lru_wa[layer].astype(BF16)
    wx_bf = lru_wx[layer].astype(BF16)
    lam = (jnp.exp(jnp.sum(lambda_q1[layer] * lambda_k1[layer]))
           - jnp.exp(jnp.sum(lambda_q2[layer] * lambda_k2[layer])) + lam_init).astype(F32)

    n = bp * s
    tm = _pick(s, 512)
    tps = s // tm
    shift1, scale1, gate1, shift2, scale2, gate2 = mods_p
    mod_map = lambda i, j: (i // tps, 0, 0)
    x2 = x_prompt.reshape(n, d)
    xrec, yrec, q, ktf, ktb, vf, vb, grec, gatt = _in_projection(
        x2, scale1, shift1, norm1_g[layer], w_seg_kt, tm, mod_map, s, True)
    rec, conv_p, lru_p = _recurrent_branch(
        xrec.reshape(bp, s, d), yrec.reshape(bp, s, d), grec.reshape(bp, s, d),
        jnp.zeros((bp, CONV_WIDTH - 1, d), F32), jnp.zeros((bp, d), F32),
        conv_w[layer], conv_b[layer], wa_bf, lru_ba[layer], wx_bf, lru_bx[layer], lru_lambda[layer],
        1, _pick(s, 256), 0)
    att = _prompt_attention(q.reshape(bp, s, d), ktb, vb.reshape(bp, s, d),
                            rel_bias_table, lam, lam_init, subln_g[layer], _pick(s, 256))
    x1 = _out_projection(rec.reshape(n, d), att.reshape(n, d), gatt, x2, gate1, w_out_bf, tm, mod_map)
    y_prompt = _ffn(x1, scale2, shift2, gate2, norm2_g[layer], final_norm_g, w_up_bf, w_down_bf, tm,
                    _pick(w_up_bf.shape[1], 1024), mod_map).reshape(bp, s, d)
    k_prompt = jnp.transpose(ktf.reshape(1, bp, 2 * N_HEADS, HEAD_DIM, s), (0, 1, 4, 2, 3))
    v_prompt = vf.reshape(1, bp, s, N_HEADS, V_DIM)

    n = bs * t
    tm = _pick(n, 512)
    assert tm % t == 0
    past = page_table.shape[1] * page
    shift1, scale1, gate1, shift2, scale2, gate2 = [
        jnp.repeat(mod[bp:bp + bs, i * d:(i + 1) * d], t, axis=0).reshape(n // tm, tm, d) for i in range(6)]
    mod_map = lambda i, j: (i, 0, 0)
    x2 = x_sample.reshape(n, d)
    xrec, yrec, q, kf, vf, grec, gatt = _in_projection(
        x2, scale1, shift1, norm1_g[layer], w_seg, tm, mod_map, t, False)
    rec, conv_s, lru_s = _recurrent_branch(
        xrec.reshape(bs, t, d), yrec.reshape(bs, t, d), grec.reshape(bs, t, d), state_conv[layer], state_lru[layer],
        conv_w[layer], conv_b[layer], wa_bf, lru_ba[layer], wx_bf, lru_bx[layer], lru_lambda[layer],
        _pick(bs, 32), t, past)
    cache_kt = jnp.transpose(cache_k[layer], (0, 2, 3, 1)).reshape(n_pool, d, page)
    cache_v2 = cache_v[layer].reshape(n_pool, page * N_HEADS, V_DIM)
    att = _sample_attention(q.reshape(bs, t, d), kf.reshape(bs, t, d), vf.reshape(bs, t, d), cache_kt, cache_v2,
                            page_table, rel_bias_table, lam, lam_init, subln_g[layer],
                            _pick(page_table.shape[1], 4))
    x1 = _out_projection(rec.reshape(n, d), att.reshape(n, d), gatt, x2, gate1, w_out_bf, tm, mod_map)
    y_sample = _ffn(x1, scale2, shift2, gate2, norm2_g[layer], final_norm_g, w_up_bf, w_down_bf, tm,
                    _pick(w_up_bf.shape[1], 1024), mod_map).reshape(bs, t, d)
    k_sample = kf.reshape(1, bs, t, 2 * N_HEADS, HEAD_DIM)
    v_sample = vf.reshape(1, bs, t, N_HEADS, V_DIM)

    return (y_prompt, y_sample, k_prompt, v_prompt,
            conv_p.reshape(1, bp, CONV_WIDTH - 1, d), lru_p.reshape(1, bp, d),
            k_sample, v_sample,
            conv_s.reshape(1, bs, CONV_WIDTH - 1, d), lru_s.reshape(1, bs, d))
```

```python
import functools
import math

import numpy as np
import jax
import jax.numpy as jnp
from jax import lax
from jax.experimental import pallas as pl
from jax.experimental.pallas import tpu as pltpu

F32 = jnp.float32
BF16 = jnp.bfloat16

N_HEADS = 8
HEAD_DIM = 64
V_DIM = 2 * HEAD_DIM
LRU_BLOCKS = 4
CONV_WIDTH = 4
LRU_C = 8.0
NUM_BUCKETS = 32
MAX_DISTANCE = 128
EPS = 1e-6
NEG_INF = -1e30
N_SEG = 7
K_SEG = 3
LANES = 128
SUBLANES = 8
VMEM_LIMIT = 56 * 1024 * 1024
ROW_CHUNK = 128
MXU_TILE = 256
SAMPLE_RING_SLOTS = 4


def _lane_tile(x, n):
    return x if n == 1 else jnp.concatenate([x] * n, axis=1)


def _cparams(sem):
    return pltpu.CompilerParams(dimension_semantics=sem, vmem_limit_bytes=VMEM_LIMIT)


def _mod_kernel(c_ref, w_ref, b_ref, o_ref):
    c = c_ref[...]
    s = (c * jax.nn.sigmoid(c)).astype(BF16)
    o_ref[...] = jnp.dot(s, w_ref[...].astype(BF16), preferred_element_type=F32) + b_ref[...]


def _modulation(c_all, w_mod, b_mod):
    m, d = c_all.shape
    n = w_mod.shape[1]
    tn = d
    return pl.pallas_call(
        _mod_kernel,
        grid=(n // tn,),
        in_specs=[pl.BlockSpec((m, d), lambda j: (0, 0)),
                  pl.BlockSpec((d, tn), lambda j: (0, j)),
                  pl.BlockSpec((1, tn), lambda j: (0, j))],
        out_specs=pl.BlockSpec((m, tn), lambda j: (0, j)),
        out_shape=jax.ShapeDtypeStruct((m, n), F32),
        compiler_params=_cparams(("arbitrary",)),
        name="modulation",
    )(c_all, w_mod, b_mod.reshape(1, n))


def _inproj_kernel(x_ref, sc_ref, sh_ref, g_ref, w_ref, *rest, keys_transposed):
    if keys_transposed:
        xrec_ref, yrec_ref, q_ref, kf_ref, kb_ref, vf_ref, vb_ref, grec_ref, gatt_ref, h_scr = rest
    else:
        xrec_ref, yrec_ref, q_ref, kf_ref, vf_ref, grec_ref, gatt_ref, h_scr = rest
    j = pl.program_id(1)

    @pl.when(j == 0)
    def _():
        x = x_ref[...]
        ms = jnp.mean(x * x, axis=-1, keepdims=True)
        y = x * lax.rsqrt(ms + EPS) * g_ref[...]
        h_scr[...] = (y * (1.0 + sc_ref[0]) + sh_ref[0]).astype(BF16)

    def project():
        return jnp.dot(h_scr[...], w_ref[...], preferred_element_type=F32)

    @pl.when(j == 0)
    def _():
        xrec_ref[...] = project()

    @pl.when(j == 1)
    def _():
        yrec_ref[...] = project()

    @pl.when(j == 2)
    def _():
        q_ref[...] = (project() * (HEAD_DIM ** -0.5)).astype(BF16)

    @pl.when(j == K_SEG)
    def _():
        if keys_transposed:
            zt = lax.dot_general(w_ref[...], h_scr[...], (((1,), (1,)), ((), ())), preferred_element_type=F32)
            kf_ref[0] = zt
            kb_ref[0] = zt.astype(BF16)
        else:
            kf_ref[...] = project()

    @pl.when(j == 4)
    def _():
        z = project()
        vf_ref[...] = z
        if keys_transposed:
            vb_ref[...] = z.astype(BF16)

    @pl.when(j == 5)
    def _():
        grec_ref[...] = project()

    @pl.when(j == 6)
    def _():
        gatt_ref[...] = project()


def _in_projection(x2, scale, shift, norm_g, w_seg, tm, mod_map, seq_len, keys_transposed):
    n, d = x2.shape
    mod_block = (1,) + scale.shape[1:]
    tok = lambda i, j: (i, 0)
    f32_out = jax.ShapeDtypeStruct((n, d), F32)
    bf_out = jax.ShapeDtypeStruct((n, d), BF16)
    out_spec = pl.BlockSpec((tm, d), tok)
    if keys_transposed:
        tps = seq_len // tm
        kt_spec = pl.BlockSpec((1, d, tm), lambda i, j: (i // tps, 0, i % tps))
        kt_f32 = jax.ShapeDtypeStruct((n // seq_len, d, seq_len), F32)
        kt_bf = jax.ShapeDtypeStruct((n // seq_len, d, seq_len), BF16)
        out_specs = [out_spec, out_spec, out_spec, kt_spec, kt_spec, out_spec, out_spec, out_spec, out_spec]
        out_shape = [f32_out, f32_out, bf_out, kt_f32, kt_bf, f32_out, bf_out, f32_out, f32_out]
    else:
        out_specs = [out_spec] * 7
        out_shape = [f32_out, f32_out, bf_out, f32_out, f32_out, f32_out, f32_out]
    return pl.pallas_call(
        functools.partial(_inproj_kernel, keys_transposed=keys_transposed),
        grid=(n // tm, N_SEG),
        in_specs=[pl.BlockSpec((tm, d), tok),
                  pl.BlockSpec(mod_block, mod_map),
                  pl.BlockSpec(mod_block, mod_map),
                  pl.BlockSpec((1, d), lambda i, j: (0, 0)),
                  pl.BlockSpec((None, d, d), lambda i, j: (j, 0, 0))],
        out_specs=out_specs,
        out_shape=out_shape,
        scratch_shapes=[pltpu.VMEM((tm, d), BF16)],
        compiler_params=_cparams(("arbitrary", "arbitrary")),
        name="in_projection",
    )(x2, scale, shift, norm_g.reshape(1, d), w_seg)


def _rec_kernel(x_ref, y_ref, g_ref, cprev_ref, hprev_ref, cw_ref, cb_ref, wa_ref, ba_ref, wx_ref, bx_ref,
                lam_ref, o_ref, cnew_ref, hnew_ref, xpad, a_scr, u_scr, hcar, *, nb, t, pos0):
    d = x_ref.shape[-1]
    ti = pl.program_id(1)
    groups = t // SUBLANES
    r = nb * groups

    @pl.when(ti == 0)
    def _():
        xpad[:, SUBLANES - (CONV_WIDTH - 1):SUBLANES, :] = cprev_ref[...]
        hcar[...] = hprev_ref[...]

    @pl.when(ti > 0)
    def _():
        xpad[:, 0:SUBLANES, :] = xpad[:, t:t + SUBLANES, :]

    x = x_ref[...]
    xpad[:, SUBLANES:, :] = x
    xc = cb_ref[...] + cw_ref[CONV_WIDTH - 1:CONV_WIDTH, :] * x
    for jj in range(CONV_WIDTH - 1):
        off = SUBLANES - (CONV_WIDTH - 1) + jj
        xc = xc + cw_ref[jj:jj + 1, :] * xpad[:, off:off + t, :]
    cnew_ref[...] = xpad[:, t + SUBLANES - (CONV_WIDTH - 1):t + SUBLANES, :]

    xc2 = xc.reshape(nb * t, d)
    xcb = xc2.astype(BF16)
    bw = d // LRU_BLOCKS
    ga = jnp.concatenate([jnp.dot(xcb[:, n * bw:(n + 1) * bw], wa_ref[n], preferred_element_type=F32)
                          for n in range(LRU_BLOCKS)], axis=-1)
    gx = jnp.concatenate([jnp.dot(xcb[:, n * bw:(n + 1) * bw], wx_ref[n], preferred_element_type=F32)
                          for n in range(LRU_BLOCKS)], axis=-1)
    gate_r = jax.nn.sigmoid(ga + ba_ref[...])
    gate_i = jax.nn.sigmoid(gx + bx_ref[...])
    z = -lam_ref[...]
    softplus = jnp.maximum(z, 0.0) + jnp.log1p(jnp.exp(-jnp.abs(z)))
    log_a = (-LRU_C) * gate_r * softplus
    a = jnp.exp(log_a)
    mult = jnp.sqrt(jnp.tanh(-log_a) * (1.0 + a * a))
    pos = pos0 + ti * t + lax.broadcasted_iota(jnp.int32, (nb, t, d), 1).reshape(nb * t, d)
    mult = jnp.where(pos == 0, 1.0, mult)
    u = mult * gate_i * xc2

    a3 = a.reshape(r, SUBLANES, d)
    u3 = u.reshape(r, SUBLANES, d)
    row = lax.broadcasted_iota(jnp.int32, (r, SUBLANES, d), 1)
    for s in (1, 2, 4):
        a_sh = pltpu.roll(a3, s, axis=1)
        u_sh = pltpu.roll(u3, s, axis=1)
        ok = row >= s
        u3 = jnp.where(ok, a3 * u_sh + u3, u3)
        a3 = jnp.where(ok, a3 * a_sh, a3)

    if groups == 1:
        h3 = a3 * hcar[...] + u3
        hcar[...] = h3[:, SUBLANES - 1:SUBLANES, :]
        h2 = h3.reshape(nb * t, d)
    else:
        assert nb == 1
        a_scr[...] = a3
        u_scr[...] = u3

        def body(gi, hin):
            hg = a_scr[gi] * hin + u_scr[gi]
            u_scr[gi] = hg
            return hg[SUBLANES - 1:SUBLANES, :]

        hcar[0] = lax.fori_loop(0, groups, body, hcar[0])
        h2 = u_scr[...].reshape(nb * t, d)
    hnew_ref[...] = hcar[...]
    yv = y_ref[...].reshape(nb * t, d)
    gv = g_ref[...].reshape(nb * t, d)
    o_ref[...] = (jax.nn.sigmoid(gv) * (h2 * jax.nn.gelu(yv))).reshape(nb, t, d)


def _recurrent_branch(xrec, yrec, grec, conv_prev, h_prev, conv_w, conv_b, wa_bf, ba, wx_bf, bx, lam, nb, t, pos0):
    b, s, d = xrec.shape
    groups = t // SUBLANES
    r = nb * groups
    blk = pl.BlockSpec((nb, t, d), lambda bi, ti: (bi, ti, 0))
    vec = pl.BlockSpec((1, d), lambda bi, ti: (0, 0))
    wblk = pl.BlockSpec(wa_bf.shape, lambda bi, ti: (0, 0, 0))
    kern = functools.partial(_rec_kernel, nb=nb, t=t, pos0=pos0)
    return pl.pallas_call(
        kern,
        grid=(b // nb, s // t),
        in_specs=[blk, blk, blk,
                  pl.BlockSpec((nb, CONV_WIDTH - 1, d), lambda bi, ti: (bi, 0, 0)),
                  pl.BlockSpec((nb, 1, d), lambda bi, ti: (bi, 0, 0)),
                  pl.BlockSpec((CONV_WIDTH, d), lambda bi, ti: (0, 0)),
                  vec, wblk, vec, wblk, vec, vec],
        out_specs=[blk,
                   pl.BlockSpec((nb, CONV_WIDTH - 1, d), lambda bi, ti: (bi, 0, 0)),
                   pl.BlockSpec((nb, 1, d), lambda bi, ti: (bi, 0, 0))],
        out_shape=[jax.ShapeDtypeStruct((b, s, d), F32),
                   jax.ShapeDtypeStruct((b, CONV_WIDTH - 1, d), F32),
                   jax.ShapeDtypeStruct((b, 1, d), F32)],
        scratch_shapes=[pltpu.VMEM((nb, t + SUBLANES, d), F32),
                        pltpu.VMEM((r, SUBLANES, d), F32),
                        pltpu.VMEM((r, SUBLANES, d), F32),
                        pltpu.VMEM((nb, 1, d), F32)],
        compiler_params=_cparams(("arbitrary", "arbitrary")),
        name="recurrent_branch",
    )(xrec, yrec, grec, conv_prev, h_prev.reshape(b, 1, d), conv_w, conv_b.reshape(1, d), wa_bf,
      ba.reshape(1, d), wx_bf, bx.reshape(1, d), lam.reshape(1, d))


def _bucket_np(rel):
    n = np.maximum(rel, 0)
    max_exact = NUM_BUCKETS // 2
    ratio = np.log(np.maximum(n, max_exact).astype(np.float32) / np.float32(max_exact)) / np.float32(
        math.log(MAX_DISTANCE / max_exact))
    large = max_exact + (ratio * np.float32(NUM_BUCKETS - max_exact)).astype(np.int32)
    large = np.minimum(large, NUM_BUCKETS - 1)
    return np.where(n < max_exact, n, large).astype(np.int32)


def _far_distance():
    far = 1
    while not np.all(_bucket_np(np.arange(far, far + 4 * MAX_DISTANCE)) == NUM_BUCKETS - 1):
        far += 1
    return far


def _bias_of_distances(dist, table):
    vals = table[jnp.asarray(_bucket_np(dist))] - table[NUM_BUCKETS - 1]
    return jnp.where(jnp.asarray(dist >= 0)[:, None], vals, NEG_INF).T


def _toeplitz_bias(base, t, table):
    period = 2 * t
    idx = np.arange(period)
    col_minus_row = np.where(idx < t, idx, idx - period)
    dist = np.where(idx == t, -1, base - col_minus_row)
    vec = _bias_of_distances(dist, table)
    skew = jnp.tile(vec, (1, t))[:, :t * (period - 1)].reshape(-1, t, period - 1)
    return skew[:, :, :t]


def _query_rows_bias(offset, t, keys, table):
    u = np.arange(keys + t - 1)
    vec = _bias_of_distances(offset + (t - 1) - u, table)
    per_query = jnp.stack([vec[:, t - 1 - tq:t - 1 - tq + keys] for tq in range(t)], axis=1)
    return jnp.repeat(per_query, 2, axis=0).reshape(2 * N_HEADS * t, keys)


def _pattn_kernel(lam_ref, q_ref, kt_ref, v_ref, bd_ref, bs_ref, g_ref, o_ref, vaug, m_scr, acc_scr, *, t, rc, lam_init):
    i = pl.program_id(2)

    @pl.when(i == 0)
    def _():
        vaug[:, :V_DIM] = v_ref[0]
        vaug[:, V_DIM:] = jnp.ones((vaug.shape[0], V_DIM), BF16)

    q = q_ref[0]
    lane = lax.broadcasted_iota(jnp.int32, q.shape, 1)
    zero = jnp.zeros_like(q)
    qz = (jnp.where(lane < HEAD_DIM, q, zero), jnp.where(lane >= HEAD_DIM, q, zero))
    m_scr[...] = jnp.full(m_scr.shape, NEG_INF, F32)
    acc_scr[...] = jnp.zeros(acc_scr.shape, F32)

    def block(kb, bias_ref, causal=False):
        start = pl.multiple_of(kb * t, t)
        kt = kt_ref[0, :, pl.ds(start, t)]
        va = vaug[pl.ds(start, t), :]
        chains = [(r0, mp) for r0 in range(0, t, rc) for mp in range(2)]

        def keys_of(r0):
            return min(t, -(-(r0 + rc) // MXU_TILE) * MXU_TILE) if causal else t

        def logits(chain):
            r0, mp = chain
            return jnp.dot(qz[mp][r0:r0 + rc], kt[:, :keys_of(r0)], preferred_element_type=F32)

        s_next = logits(chains[0])
        for n, (r0, mp) in enumerate(chains):
            rows = slice(r0, r0 + rc)
            nk = keys_of(r0)
            s = s_next
            if n + 1 < len(chains):
                s_next = logits(chains[n + 1])
            if bias_ref is not None:
                s = s + bias_ref[0, rows, :nk]
            m_old = m_scr[mp, rows]
            m_new = jnp.maximum(m_old, jnp.max(s, axis=-1, keepdims=True))
            p = jnp.exp(s - _lane_tile(m_new, nk // LANES)).astype(BF16)
            alpha = jnp.exp(m_old - m_new)
            acc_scr[mp, rows] = (_lane_tile(alpha, 2) * acc_scr[mp, rows]
                                 + jnp.dot(p, va[:nk], preferred_element_type=F32))
            m_scr[mp, rows] = m_new

    def far(kb, carry):
        block(kb, None)
        return carry

    lax.fori_loop(0, jnp.maximum(i - 1, 0), far, 0)

    @pl.when(i >= 1)
    def _():
        block(i - 1, bs_ref)

    block(i, bd_ref, causal=True)

    lam = lam_ref[0]
    o = (acc_scr[0, :, :V_DIM] / acc_scr[0, :, V_DIM:]
         - lam * (acc_scr[1, :, :V_DIM] / acc_scr[1, :, V_DIM:]))
    ms = jnp.mean(o * o, axis=-1, keepdims=True)
    o_ref[0] = o * lax.rsqrt(ms + EPS) * g_ref[...] * (1.0 - lam_init)


def _prompt_attention(q, kt, v, rel_table, lam, lam_init, subln_g, t):
    b, s, d = q.shape
    far = _far_distance()
    assert t + 1 >= far, "blocks two or more below the diagonal must see a constant bias"
    bias_diag = _toeplitz_bias(0, t, rel_table)
    bias_sub = _toeplitz_bias(t, t, rel_table)
    kern = functools.partial(_pattn_kernel, t=t, rc=min(t, ROW_CHUNK), lam_init=lam_init)
    return pl.pallas_call(
        kern,
        grid=(b, N_HEADS, s // t),
        in_specs=[pl.BlockSpec(memory_space=pltpu.SMEM),
                  pl.BlockSpec((1, t, V_DIM), lambda bi, h, i: (bi, i, h)),
                  pl.BlockSpec((1, V_DIM, s), lambda bi, h, i: (bi, h, 0)),
                  pl.BlockSpec((1, s, V_DIM), lambda bi, h, i: (bi, 0, h)),
                  pl.BlockSpec((1, t, t), lambda bi, h, i: (h, 0, 0)),
                  pl.BlockSpec((1, t, t), lambda bi, h, i: (h, 0, 0)),
                  pl.BlockSpec((1, V_DIM), lambda bi, h, i: (0, 0))],
        out_specs=pl.BlockSpec((1, t, V_DIM), lambda bi, h, i: (bi, i, h)),
        out_shape=jax.ShapeDtypeStruct((b, s, d), F32),
        scratch_shapes=[pltpu.VMEM((s, 2 * V_DIM), BF16),
                        pltpu.VMEM((2, t, LANES), F32),
                        pltpu.VMEM((2, t, 2 * V_DIM), F32)],
        compiler_params=_cparams(("arbitrary", "arbitrary", "arbitrary")),
        name="prompt_attention",
    )(lam.reshape(1), q, kt, v, bias_diag, bias_sub, subln_g.reshape(1, V_DIM))


def _sattn_kernel(pt_ref, lam_ref, q_ref, knew_ref, vnew_ref, blast_ref, bnew_ref, g_ref, *rest, gp, nbuf, lam_init):
    ck_hbm, cv_hbm, o_ref, kbuf, vbuf, sem, qbd, knew_t, vpad, m_scr, l_scr, acc = rest
    gi = pl.program_id(1)
    n_groups = pl.num_programs(1)
    n_maps = 2 * N_HEADS
    t = q_ref.shape[1]
    d = q_ref.shape[2]
    rows = n_maps * t
    page = kbuf.shape[3]

    step = pl.program_id(0) * n_groups + gi
    total = pl.num_programs(0) * n_groups

    def page_copies(st, slot):
        sb = st // n_groups
        sg = st % n_groups
        out = []
        for jj in range(gp):
            pg = pt_ref[sb, sg * gp + jj]
            out.append(pltpu.make_async_copy(ck_hbm.at[pg], kbuf.at[slot, jj], sem.at[slot, 0, jj]))
            out.append(pltpu.make_async_copy(cv_hbm.at[pg], vbuf.at[slot, jj], sem.at[slot, 1, jj]))
        return out

    @pl.when(step == 0)
    def _():
        for st in range(nbuf - 1):
            for c in page_copies(st, st):
                c.start()

    ahead = step + (nbuf - 1)

    @pl.when(ahead < total)
    def _():
        for c in page_copies(ahead, ahead % nbuf):
            c.start()

    slot = step % nbuf
    for c in page_copies(step, slot):
        c.wait()

    @pl.when(gi == 0)
    def _():
        qf = q_ref[0].astype(F32)
        qt = jnp.concatenate([qf] * n_maps, axis=0)
        rmap = lax.broadcasted_iota(jnp.int32, (rows, d), 0) // t
        cmap = lax.broadcasted_iota(jnp.int32, (rows, d), 1) // HEAD_DIM
        qbd[...] = jnp.where(rmap == cmap, qt, 0.0).astype(BF16)
        pad = jnp.zeros((page - t, d), F32)
        knew_t[...] = jnp.concatenate([knew_ref[0], pad], axis=0).T.astype(BF16)
        vpad[...] = jnp.concatenate([vnew_ref[0], pad], axis=0).astype(BF16)
        m_scr[...] = jnp.full(m_scr.shape, NEG_INF, F32)
        l_scr[...] = jnp.zeros(l_scr.shape, F32)
        acc[...] = jnp.zeros(acc.shape, F32)

    def update(kt, v_heads, bias):
        n = kt.shape[1]
        s = jnp.dot(qbd[...], kt, preferred_element_type=F32)
        if bias is not None:
            s = s + bias
        m_old = m_scr[...]
        m_new = jnp.maximum(m_old, jnp.max(s, axis=-1, keepdims=True))
        p = jnp.exp(s - _lane_tile(m_new, n // LANES))
        alpha = jnp.exp(m_old - m_new)
        l_scr[...] = alpha * l_scr[...] + jnp.sum(p, axis=-1, keepdims=True)
        m_scr[...] = m_new
        pb = p.astype(BF16)
        pv = jnp.concatenate([jnp.dot(pb[2 * t * h:2 * t * (h + 1), :], v_heads[h], preferred_element_type=F32)
                              for h in range(N_HEADS)], axis=0)
        acc[...] = alpha * acc[...] + pv

    def pages():
        kt = jnp.concatenate([kbuf[slot, jj].astype(BF16) for jj in range(gp)], axis=1)
        v_heads = [jnp.concatenate([vbuf[slot, jj, pl.ds(h, page, stride=N_HEADS), :].astype(BF16)
                                    for jj in range(gp)], axis=0)
                   for h in range(N_HEADS)]
        return kt, v_heads

    last = pl.num_programs(1) - 1

    @pl.when(gi < last)
    def _():
        kt, v_heads = pages()
        update(kt, v_heads, None)

    @pl.when(gi == last)
    def _():
        kt, v_heads = pages()
        update(kt, v_heads, blast_ref[...])
        update(knew_t[...], [vpad[:, h * V_DIM:(h + 1) * V_DIM] for h in range(N_HEADS)], bnew_ref[...])
        lam = lam_ref[0]
        on = acc[...] / l_scr[...]
        for h in range(N_HEADS):
            r0 = 2 * h * t
            o = on[r0:r0 + t, :] - lam * on[r0 + t:r0 + 2 * t, :]
            ms = jnp.mean(o * o, axis=-1, keepdims=True)
            o_ref[0, :, h * V_DIM:(h + 1) * V_DIM] = o * lax.rsqrt(ms + EPS) * g_ref[...] * (1.0 - lam_init)


def _sample_attention(q, k_new, v_new, cache_kt, cache_v2, page_table, rel_table, lam, lam_init, subln_g, gp):
    b, t, d = q.shape
    page = cache_kt.shape[2]
    n_pages = page_table.shape[1]
    past = n_pages * page
    rows = 2 * N_HEADS * t
    assert rows == LANES and V_DIM == LANES and page == LANES and n_pages % gp == 0
    far = _far_distance()
    keys = gp * page
    assert past - ((n_pages - gp) * page - 1) >= far, "all page groups but the last must see a constant bias"
    bias_last = _query_rows_bias(keys, t, keys, rel_table)
    bias_new = _query_rows_bias(0, t, page, rel_table)

    seq = lambda bi, gi, pt: (bi, 0, 0)
    fixed = lambda bi, gi, pt: (0, 0)
    nbuf = SAMPLE_RING_SLOTS
    assert b * (n_pages // gp) >= nbuf - 1
    kern = functools.partial(_sattn_kernel, gp=gp, nbuf=nbuf, lam_init=lam_init)
    grid_spec = pltpu.PrefetchScalarGridSpec(
        num_scalar_prefetch=1,
        grid=(b, n_pages // gp),
        in_specs=[pl.BlockSpec(memory_space=pltpu.SMEM),
                  pl.BlockSpec((1, t, d), seq),
                  pl.BlockSpec((1, t, d), seq),
                  pl.BlockSpec((1, t, d), seq),
                  pl.BlockSpec((rows, keys), fixed),
                  pl.BlockSpec((rows, page), fixed),
                  pl.BlockSpec((1, V_DIM), fixed),
                  pl.BlockSpec(memory_space=pl.ANY),
                  pl.BlockSpec(memory_space=pl.ANY)],
        out_specs=pl.BlockSpec((1, t, d), seq),
        scratch_shapes=[pltpu.VMEM((nbuf, gp, d, page), F32),
                        pltpu.VMEM((nbuf, gp, page * N_HEADS, V_DIM), F32),
                        pltpu.SemaphoreType.DMA((nbuf, 2, gp)),
                        pltpu.VMEM((rows, d), BF16),
                        pltpu.VMEM((d, page), BF16),
                        pltpu.VMEM((page, d), BF16),
                        pltpu.VMEM((rows, LANES), F32),
                        pltpu.VMEM((rows, LANES), F32),
                        pltpu.VMEM((rows, V_DIM), F32)])
    return pl.pallas_call(
        kern,
        grid_spec=grid_spec,
        out_shape=jax.ShapeDtypeStruct((b, t, d), F32),
        compiler_params=_cparams(("arbitrary", "arbitrary")),
        name="sample_attention",
    )(page_table, lam.reshape(1), q, k_new, v_new, bias_last, bias_new, subln_g.reshape(1, V_DIM),
      cache_kt, cache_v2)


def _outproj_kernel(rec_ref, att_ref, gatt_ref, x_ref, gate_ref, w_ref, o_ref):
    merged = rec_ref[...] + jax.nn.sigmoid(gatt_ref[...]) * att_ref[...]
    y = jnp.dot(merged.astype(BF16), w_ref[...], preferred_element_type=F32)
    o_ref[...] = x_ref[...] + gate_ref[0] * y


def _out_projection(rec, att, gatt, x2, gate, w_out_bf, tm, mod_map):
    n, d = x2.shape
    tok = pl.BlockSpec((tm, d), lambda i: (i, 0))
    mod_block = (1,) + gate.shape[1:]
    return pl.pallas_call(
        _outproj_kernel,
        grid=(n // tm,),
        in_specs=[tok, tok, tok, tok,
                  pl.BlockSpec(mod_block, lambda i: mod_map(i, 0)),
                  pl.BlockSpec((d, d), lambda i: (0, 0))],
        out_specs=tok,
        out_shape=jax.ShapeDtypeStruct((n, d), F32),
        compiler_params=_cparams(("arbitrary",)),
        name="out_projection",
    )(rec, att, gatt, x2, gate, w_out_bf)


def _ffn_kernel(x_ref, sc_ref, sh_ref, gate_ref, g2_ref, gf_ref, wu_ref, wd_ref, o_ref, h_scr, acc_scr):
    f = pl.program_id(1)

    @pl.when(f == 0)
    def _():
        x = x_ref[...]
        ms = jnp.mean(x * x, axis=-1, keepdims=True)
        y = x * lax.rsqrt(ms + EPS) * g2_ref[...]
        h_scr[...] = (y * (1.0 + sc_ref[0]) + sh_ref[0]).astype(BF16)
        acc_scr[...] = jnp.zeros(acc_scr.shape, F32)

    up = jnp.dot(h_scr[...], wu_ref[...], preferred_element_type=F32)
    act = jnp.square(jnp.maximum(up, 0.0)).astype(BF16)
    acc_scr[...] += jnp.dot(act, wd_ref[...], preferred_element_type=F32)

    @pl.when(f == pl.num_programs(1) - 1)
    def _():
        x = x_ref[...] + gate_ref[0] * acc_scr[...]
        ms = jnp.mean(x * x, axis=-1, keepdims=True)
        o_ref[...] = x * lax.rsqrt(ms + EPS) * gf_ref[...]


def _ffn(x2, scale, shift, gate, norm2_g, final_g, w_up_bf, w_down_bf, tm, tf, mod_map):
    n, d = x2.shape
    dff = w_up_bf.shape[1]
    tok = pl.BlockSpec((tm, d), lambda i, f: (i, 0))
    mod_block = (1,) + scale.shape[1:]
    mod_spec = pl.BlockSpec(mod_block, mod_map)
    vec = pl.BlockSpec((1, d), lambda i, f: (0, 0))
    return pl.pallas_call(
        _ffn_kernel,
        grid=(n // tm, dff // tf),
        in_specs=[tok, mod_spec, mod_spec, mod_spec, vec, vec,
                  pl.BlockSpec((d, tf), lambda i, f: (0, f)),
                  pl.BlockSpec((tf, d), lambda i, f: (f, 0))],
        out_specs=tok,
        out_shape=jax.ShapeDtypeStruct((n, d), F32),
        scratch_shapes=[pltpu.VMEM((tm, d), BF16), pltpu.VMEM((tm, d), F32)],
        compiler_params=_cparams(("arbitrary", "arbitrary")),
        name="ffn",
    )(x2, scale, shift, gate, norm2_g.reshape(1, d), final_g.reshape(1, d), w_up_bf, w_down_bf)


def _pick(n, pref):
    t = min(n, pref)
    assert n % t == 0
    return t


def kernel(x_prompt, x_sample, cache_k, cache_v, state_conv, state_lru, page_table, c_prompt, c_sample, rel_bias_table, norm1_g, norm2_g, w_mod, b_mod, w_in, conv_w, conv_b, lru_wa, lru_ba, lru_wx, lru_bx, lru_lambda, lambda_q1, lambda_k1, lambda_q2, lambda_k2, subln_g, w_out, w_up, w_down, final_norm_g):
    depth = w_in.shape[0]
    assert depth == 1
    layer = 0
    lam_init = 0.8 - 0.6 * math.exp(-0.3 * layer)
    bp, s, d = x_prompt.shape
    bs, t, _ = x_sample.shape
    n_pool, page = cache_k.shape[1], cache_k.shape[2]

    n_seq = bp + bs
    n_pad = -n_seq % (2 * SUBLANES)
    c_all = jnp.concatenate([c_prompt, c_sample, jnp.zeros((n_pad, d), F32)], axis=0)
    mod = _modulation(c_all, w_mod[layer], b_mod[layer])
    mods_p = [mod[:bp, i * d:(i + 1) * d].reshape(bp, 1, d) for i in range(6)]

    w_seg = jnp.transpose(w_in[layer].astype(BF16).reshape(d, N_SEG, d), (1, 0, 2))
    w_seg_kt = w_seg.at[K_SEG].set(w_seg[K_SEG].T)
    w_out_bf = w_out[layer].astype(BF16)
    w_up_bf = w_up[layer].astype(BF16)
    w_down_bf = w_down[layer].astype(BF16)
    wa_bf = lru_wa[layer].astype(BF16)
    wx_bf = lru_wx[layer].astype(BF16)
    lam = (jnp.exp(jnp.sum(lambda_q1[layer] * lambda_k1[layer]))
           - jnp.exp(jnp.sum(lambda_q2[layer] * lambda_k2[layer])) + lam_init).astype(F32)

    n = bp * s
    tm = _pick(s, 512)
    tps = s // tm
    shift1, scale1, gate1, shift2, scale2, gate2 = mods_p
    mod_map = lambda i, j: (i // tps, 0, 0)
    x2 = x_prompt.reshape(n, d)
    xrec, yrec, q, ktf, ktb, vf, vb, grec, gatt = _in_projection(
        x2, scale1, shift1, norm1_g[layer], w_seg_kt, tm, mod_map, s, True)
    rec, conv_p, lru_p = _recurrent_branch(
        xrec.reshape(bp, s, d), yrec.reshape(bp, s, d), grec.reshape(bp, s, d),
        jnp.zeros((bp, CONV_WIDTH - 1, d), F32), jnp.zeros((bp, d), F32),
        conv_w[layer], conv_b[layer], wa_bf, lru_ba[layer], wx_bf, lru_bx[layer], lru_lambda[layer],
        1, _pick(s, 256), 0)
    att = _prompt_attention(q.reshape(bp, s, d), ktb, vb.reshape(bp, s, d),
                            rel_bias_table, lam, lam_init, subln_g[layer], _pick(s, 512))
    x1 = _out_projection(rec.reshape(n, d), att.reshape(n, d), gatt, x2, gate1, w_out_bf, tm, mod_map)
    y_prompt = _ffn(x1, scale2, shift2, gate2, norm2_g[layer], final_norm_g, w_up_bf, w_down_bf, tm,
                    _pick(w_up_bf.shape[1], 1024), mod_map).reshape(bp, s, d)
    k_prompt = jnp.transpose(ktf.reshape(1, bp, 2 * N_HEADS, HEAD_DIM, s), (0, 1, 4, 2, 3))
    v_prompt = vf.reshape(1, bp, s, N_HEADS, V_DIM)

    n = bs * t
    tm = _pick(n, 512)
    assert tm % t == 0
    past = page_table.shape[1] * page
    shift1, scale1, gate1, shift2, scale2, gate2 = [
        jnp.repeat(mod[bp:bp + bs, i * d:(i + 1) * d], t, axis=0).reshape(n // tm, tm, d) for i in range(6)]
    mod_map = lambda i, j: (i, 0, 0)
    x2 = x_sample.reshape(n, d)
    xrec, yrec, q, kf, vf, grec, gatt = _in_projection(
        x2, scale1, shift1, norm1_g[layer], w_seg, tm, mod_map, t, False)
    rec, conv_s, lru_s = _recurrent_branch(
        xrec.reshape(bs, t, d), yrec.reshape(bs, t, d), grec.reshape(bs, t, d), state_conv[layer], state_lru[layer],
        conv_w[layer], conv_b[layer], wa_bf, lru_ba[layer], wx_bf, lru_bx[layer], lru_lambda[layer],
        _pick(bs, 32), t, past)
    cache_kt = jnp.transpose(cache_k[layer], (0, 2, 3, 1)).reshape(n_pool, d, page)
    cache_v2 = cache_v[layer].reshape(n_pool, page * N_HEADS, V_DIM)
    att = _sample_attention(q.reshape(bs, t, d), kf.reshape(bs, t, d), vf.reshape(bs, t, d), cache_kt, cache_v2,
                            page_table, rel_bias_table, lam, lam_init, subln_g[layer],
                            _pick(page_table.shape[1], 4))
    x1 = _out_projection(rec.reshape(n, d), att.reshape(n, d), gatt, x2, gate1, w_out_bf, tm, mod_map)
    y_sample = _ffn(x1, scale2, shift2, gate2, norm2_g[layer], final_norm_g, w_up_bf, w_down_bf, tm,
                    _pick(w_up_bf.shape[1], 1024), mod_map).reshape(bs, t, d)
    k_sample = kf.reshape(1, bs, t, 2 * N_HEADS, HEAD_DIM)
    v_sample = vf.reshape(1, bs, t, N_HEADS, V_DIM)

    return (y_prompt, y_sample, k_prompt, v_prompt,
            conv_p.reshape(1, bp, CONV_WIDTH - 1, d), lru_p.reshape(1, bp, d),
            k_sample, v_sample,
            conv_s.reshape(1, bs, CONV_WIDTH - 1, d), lru_s.reshape(1, bs, d))
```

```python
import functools
import math

import numpy as np
import jax
import jax.numpy as jnp
from jax import lax
from jax.experimental import pallas as pl
from jax.experimental.pallas import tpu as pltpu

F32 = jnp.float32
BF16 = jnp.bfloat16

N_HEADS = 8
HEAD_DIM = 64
V_DIM = 2 * HEAD_DIM
LRU_BLOCKS = 4
CONV_WIDTH = 4
LRU_C = 8.0
NUM_BUCKETS = 32
MAX_DISTANCE = 128
EPS = 1e-6
NEG_INF = -1e30
N_SEG = 7
K_SEG = 3
LANES = 128
SUBLANES = 8
VMEM_LIMIT = 56 * 1024 * 1024
ROW_CHUNK = 128
MXU_TILE = 256
SAMPLE_RING_SLOTS = 4


def _lane_tile(x, n):
    return x if n == 1 else jnp.concatenate([x] * n, axis=1)


def _cparams(sem):
    return pltpu.CompilerParams(dimension_semantics=sem, vmem_limit_bytes=VMEM_LIMIT)


def _mod_kernel(c_ref, w_ref, b_ref, o_ref):
    c = c_ref[...]
    s = (c * jax.nn.sigmoid(c)).astype(BF16)
    o_ref[...] = jnp.dot(s, w_ref[...].astype(BF16), preferred_element_type=F32) + b_ref[...]


def _modulation(c_all, w_mod, b_mod):
    m, d = c_all.shape
    n = w_mod.shape[1]
    tn = d
    return pl.pallas_call(
        _mod_kernel,
        grid=(n // tn,),
        in_specs=[pl.BlockSpec((m, d), lambda j: (0, 0)),
                  pl.BlockSpec((d, tn), lambda j: (0, j)),
                  pl.BlockSpec((1, tn), lambda j: (0, j))],
        out_specs=pl.BlockSpec((m, tn), lambda j: (0, j)),
        out_shape=jax.ShapeDtypeStruct((m, n), F32),
        compiler_params=_cparams(("arbitrary",)),
        name="modulation",
    )(c_all, w_mod, b_mod.reshape(1, n))


def _inproj_kernel(x_ref, sc_ref, sh_ref, g_ref, w_ref, *rest, keys_transposed):
    if keys_transposed:
        xrec_ref, yrec_ref, q_ref, kf_ref, kb_ref, vf_ref, vb_ref, grec_ref, gatt_ref, h_scr = rest
    else:
        xrec_ref, yrec_ref, q_ref, kf_ref, vf_ref, grec_ref, gatt_ref, h_scr = rest
    j = pl.program_id(1)

    @pl.when(j == 0)
    def _():
        x = x_ref[...]
        ms = jnp.mean(x * x, axis=-1, keepdims=True)
        y = x * lax.rsqrt(ms + EPS) * g_ref[...]
        h_scr[...] = (y * (1.0 + sc_ref[0]) + sh_ref[0]).astype(BF16)

    def project():
        return jnp.dot(h_scr[...], w_ref[...], preferred_element_type=F32)

    @pl.when(j == 0)
    def _():
        xrec_ref[...] = project()

    @pl.when(j == 1)
    def _():
        yrec_ref[...] = project()

    @pl.when(j == 2)
    def _():
        q_ref[...] = (project() * (HEAD_DIM ** -0.5)).astype(BF16)

    @pl.when(j == K_SEG)
    def _():
        if keys_transposed:
            zt = lax.dot_general(w_ref[...], h_scr[...], (((1,), (1,)), ((), ())), preferred_element_type=F32)
            kf_ref[0] = zt
            kb_ref[0] = zt.astype(BF16)
        else:
            kf_ref[...] = project()

    @pl.when(j == 4)
    def _():
        z = project()
        vf_ref[...] = z
        if keys_transposed:
            vb_ref[...] = z.astype(BF16)

    @pl.when(j == 5)
    def _():
        grec_ref[...] = project()

    @pl.when(j == 6)
    def _():
        gatt_ref[...] = project()


def _in_projection(x2, scale, shift, norm_g, w_seg, tm, mod_map, seq_len, keys_transposed):
    n, d = x2.shape
    mod_block = (1,) + scale.shape[1:]
    tok = lambda i, j: (i, 0)
    f32_out = jax.ShapeDtypeStruct((n, d), F32)
    bf_out = jax.ShapeDtypeStruct((n, d), BF16)
    out_spec = pl.BlockSpec((tm, d), tok)
    if keys_transposed:
        tps = seq_len // tm
        kt_spec = pl.BlockSpec((1, d, tm), lambda i, j: (i // tps, 0, i % tps))
        kt_f32 = jax.ShapeDtypeStruct((n // seq_len, d, seq_len), F32)
        kt_bf = jax.ShapeDtypeStruct((n // seq_len, d, seq_len), BF16)
        out_specs = [out_spec, out_spec, out_spec, kt_spec, kt_spec, out_spec, out_spec, out_spec, out_spec]
        out_shape = [f32_out, f32_out, bf_out, kt_f32, kt_bf, f32_out, bf_out, f32_out, f32_out]
    else:
        out_specs = [out_spec] * 7
        out_shape = [f32_out, f32_out, bf_out, f32_out, f32_out, f32_out, f32_out]
    return pl.pallas_call(
        functools.partial(_inproj_kernel, keys_transposed=keys_transposed),
        grid=(n // tm, N_SEG),
        in_specs=[pl.BlockSpec((tm, d), tok),
                  pl.BlockSpec(mod_block, mod_map),
                  pl.BlockSpec(mod_block, mod_map),
                  pl.BlockSpec((1, d), lambda i, j: (0, 0)),
                  pl.BlockSpec((None, d, d), lambda i, j: (j, 0, 0))],
        out_specs=out_specs,
        out_shape=out_shape,
        scratch_shapes=[pltpu.VMEM((tm, d), BF16)],
        compiler_params=_cparams(("arbitrary", "arbitrary")),
        name="in_projection",
    )(x2, scale, shift, norm_g.reshape(1, d), w_seg)


def _rec_kernel(x_ref, y_ref, g_ref, cprev_ref, hprev_ref, cw_ref, cb_ref, wa_ref, ba_ref, wx_ref, bx_ref,
                lam_ref, o_ref, cnew_ref, hnew_ref, xpad, a_scr, u_scr, hcar, *, nb, t, pos0):
    d = x_ref.shape[-1]
    ti = pl.program_id(1)
    groups = t // SUBLANES
    r = nb * groups

    @pl.when(ti == 0)
    def _():
        xpad[:, SUBLANES - (CONV_WIDTH - 1):SUBLANES, :] = cprev_ref[...]
        hcar[...] = hprev_ref[...]

    @pl.when(ti > 0)
    def _():
        xpad[:, 0:SUBLANES, :] = xpad[:, t:t + SUBLANES, :]

    x = x_ref[...]
    xpad[:, SUBLANES:, :] = x
    xc = cb_ref[...] + cw_ref[CONV_WIDTH - 1:CONV_WIDTH, :] * x
    for jj in range(CONV_WIDTH - 1):
        off = SUBLANES - (CONV_WIDTH - 1) + jj
        xc = xc + cw_ref[jj:jj + 1, :] * xpad[:, off:off + t, :]
    cnew_ref[...] = xpad[:, t + SUBLANES - (CONV_WIDTH - 1):t + SUBLANES, :]

    xc2 = xc.reshape(nb * t, d)
    xcb = xc2.astype(BF16)
    bw = d // LRU_BLOCKS
    ga = jnp.concatenate([jnp.dot(xcb[:, n * bw:(n + 1) * bw], wa_ref[n], preferred_element_type=F32)
                          for n in range(LRU_BLOCKS)], axis=-1)
    gx = jnp.concatenate([jnp.dot(xcb[:, n * bw:(n + 1) * bw], wx_ref[n], preferred_element_type=F32)
                          for n in range(LRU_BLOCKS)], axis=-1)
    gate_r = jax.nn.sigmoid(ga + ba_ref[...])
    gate_i = jax.nn.sigmoid(gx + bx_ref[...])
    z = -lam_ref[...]
    softplus = jnp.maximum(z, 0.0) + jnp.log1p(jnp.exp(-jnp.abs(z)))
    log_a = (-LRU_C) * gate_r * softplus
    a = jnp.exp(log_a)
    mult = jnp.sqrt(jnp.tanh(-log_a) * (1.0 + a * a))
    pos = pos0 + ti * t + lax.broadcasted_iota(jnp.int32, (nb, t, d), 1).reshape(nb * t, d)
    mult = jnp.where(pos == 0, 1.0, mult)
    u = mult * gate_i * xc2

    a3 = a.reshape(r, SUBLANES, d)
    u3 = u.reshape(r, SUBLANES, d)
    row = lax.broadcasted_iota(jnp.int32, (r, SUBLANES, d), 1)
    for s in (1, 2, 4):
        a_sh = pltpu.roll(a3, s, axis=1)
        u_sh = pltpu.roll(u3, s, axis=1)
        ok = row >= s
        u3 = jnp.where(ok, a3 * u_sh + u3, u3)
        a3 = jnp.where(ok, a3 * a_sh, a3)

    if groups == 1:
        h3 = a3 * hcar[...] + u3
        hcar[...] = h3[:, SUBLANES - 1:SUBLANES, :]
        h2 = h3.reshape(nb * t, d)
    else:
        assert nb == 1
        a_scr[...] = a3
        u_scr[...] = u3

        def body(gi, hin):
            hg = a_scr[gi] * hin + u_scr[gi]
            u_scr[gi] = hg
            return hg[SUBLANES - 1:SUBLANES, :]

        hcar[0] = lax.fori_loop(0, groups, body, hcar[0])
        h2 = u_scr[...].reshape(nb * t, d)
    hnew_ref[...] = hcar[...]
    yv = y_ref[...].reshape(nb * t, d)
    gv = g_ref[...].reshape(nb * t, d)
    o_ref[...] = (jax.nn.sigmoid(gv) * (h2 * jax.nn.gelu(yv))).reshape(nb, t, d)


def _recurrent_branch(xrec, yrec, grec, conv_prev, h_prev, conv_w, conv_b, wa_bf, ba, wx_bf, bx, lam, nb, t, pos0):
    b, s, d = xrec.shape
    groups = t // SUBLANES
    r = nb * groups
    blk = pl.BlockSpec((nb, t, d), lambda bi, ti: (bi, ti, 0))
    vec = pl.BlockSpec((1, d), lambda bi, ti: (0, 0))
    wblk = pl.BlockSpec(wa_bf.shape, lambda bi, ti: (0, 0, 0))
    kern = functools.partial(_rec_kernel, nb=nb, t=t, pos0=pos0)
    return pl.pallas_call(
        kern,
        grid=(b // nb, s // t),
        in_specs=[blk, blk, blk,
                  pl.BlockSpec((nb, CONV_WIDTH - 1, d), lambda bi, ti: (bi, 0, 0)),
                  pl.BlockSpec((nb, 1, d), lambda bi, ti: (bi, 0, 0)),
                  pl.BlockSpec((CONV_WIDTH, d), lambda bi, ti: (0, 0)),
                  vec, wblk, vec, wblk, vec, vec],
        out_specs=[blk,
                   pl.BlockSpec((nb, CONV_WIDTH - 1, d), lambda bi, ti: (bi, 0, 0)),
                   pl.BlockSpec((nb, 1, d), lambda bi, ti: (bi, 0, 0))],
        out_shape=[jax.ShapeDtypeStruct((b, s, d), F32),
                   jax.ShapeDtypeStruct((b, CONV_WIDTH - 1, d), F32),
                   jax.ShapeDtypeStruct((b, 1, d), F32)],
        scratch_shapes=[pltpu.VMEM((nb, t + SUBLANES, d), F32),
                        pltpu.VMEM((r, SUBLANES, d), F32),
                        pltpu.VMEM((r, SUBLANES, d), F32),
                        pltpu.VMEM((nb, 1, d), F32)],
        compiler_params=_cparams(("arbitrary", "arbitrary")),
        name="recurrent_branch",
    )(xrec, yrec, grec, conv_prev, h_prev.reshape(b, 1, d), conv_w, conv_b.reshape(1, d), wa_bf,
      ba.reshape(1, d), wx_bf, bx.reshape(1, d), lam.reshape(1, d))


def _bucket_np(rel):
    n = np.maximum(rel, 0)
    max_exact = NUM_BUCKETS // 2
    ratio = np.log(np.maximum(n, max_exact).astype(np.float32) / np.float32(max_exact)) / np.float32(
        math.log(MAX_DISTANCE / max_exact))
    large = max_exact + (ratio * np.float32(NUM_BUCKETS - max_exact)).astype(np.int32)
    large = np.minimum(large, NUM_BUCKETS - 1)
    return np.where(n < max_exact, n, large).astype(np.int32)


def _far_distance():
    far = 1
    while not np.all(_bucket_np(np.arange(far, far + 4 * MAX_DISTANCE)) == NUM_BUCKETS - 1):
        far += 1
    return far


def _bias_of_distances(dist, table):
    vals = table[jnp.asarray(_bucket_np(dist))] - table[NUM_BUCKETS - 1]
    return jnp.where(jnp.asarray(dist >= 0)[:, None], vals, NEG_INF).T


def _toeplitz_bias(base, t, table):
    period = 2 * t
    idx = np.arange(period)
    col_minus_row = np.where(idx < t, idx, idx - period)
    dist = np.where(idx == t, -1, base - col_minus_row)
    vec = _bias_of_distances(dist, table)
    skew = jnp.tile(vec, (1, t))[:, :t * (period - 1)].reshape(-1, t, period - 1)
    return skew[:, :, :t]


def _query_rows_bias(offset, t, keys, table):
    u = np.arange(keys + t - 1)
    vec = _bias_of_distances(offset + (t - 1) - u, table)
    per_query = jnp.stack([vec[:, t - 1 - tq:t - 1 - tq + keys] for tq in range(t)], axis=1)
    return jnp.repeat(per_query, 2, axis=0).reshape(2 * N_HEADS * t, keys)


def _prompt_unit(kb, i, lam, q_ref, kt_ref, v_ref, bias_ref, g_ref, o_ref, vaug, m_scr, acc_scr, *, t, rc, lam_init):
    @pl.when((i == 0) & (kb == 0))
    def _():
        vaug[:, :V_DIM] = v_ref[0]
        vaug[:, V_DIM:] = jnp.ones((vaug.shape[0], V_DIM), BF16)

    @pl.when(kb == 0)
    def _():
        m_scr[...] = jnp.full(m_scr.shape, NEG_INF, F32)
        acc_scr[...] = jnp.zeros(acc_scr.shape, F32)

    q = q_ref[0]
    lane = lax.broadcasted_iota(jnp.int32, q.shape, 1)
    zero = jnp.zeros_like(q)
    qz = (jnp.where(lane < HEAD_DIM, q, zero), jnp.where(lane >= HEAD_DIM, q, zero))
    start = pl.multiple_of(kb * t, t)

    def add_bias(s, base, r0):
        tiles = []
        for c0 in range(0, s.shape[1], rc):
            tile = s[:, c0:c0 + rc]
            delta = base + r0 - c0
            if delta < 0:
                tile = jnp.full(tile.shape, NEG_INF, F32)
            elif delta < 2 * rc:
                tile = tile + bias_ref[0, delta // rc]
            tiles.append(tile)
        return jnp.concatenate(tiles, axis=1)

    def block(base):
        causal = base == 0
        kt = kt_ref[0, :, pl.ds(start, t)]
        va = vaug[pl.ds(start, t), :]
        chains = [(r0, mp) for r0 in range(0, t, rc) for mp in range(2)]

        def keys_of(r0):
            return min(t, -(-(r0 + rc) // MXU_TILE) * MXU_TILE) if causal else t

        def logits(chain):
            r0, mp = chain
            return jnp.dot(qz[mp][r0:r0 + rc], kt[:, :keys_of(r0)], preferred_element_type=F32)

        s_next = logits(chains[0])
        for n, (r0, mp) in enumerate(chains):
            rows = slice(r0, r0 + rc)
            nk = keys_of(r0)
            s = s_next
            if n + 1 < len(chains):
                s_next = logits(chains[n + 1])
            if base is not None:
                s = add_bias(s, base, r0)
            m_old = m_scr[mp, rows]
            m_new = jnp.maximum(m_old, jnp.max(s, axis=-1, keepdims=True))
            p = jnp.exp(s - _lane_tile(m_new, nk // LANES)).astype(BF16)
            alpha = jnp.exp(m_old - m_new)
            acc_scr[mp, rows] = (_lane_tile(alpha, 2) * acc_scr[mp, rows]
                                 + jnp.dot(p, va[:nk], preferred_element_type=F32))
            m_scr[mp, rows] = m_new

    @pl.when(kb < i - 1)
    def _():
        block(None)

    @pl.when(kb == i - 1)
    def _():
        block(t)

    @pl.when(kb == i)
    def _():
        block(0)
        o = (acc_scr[0, :, :V_DIM] / acc_scr[0, :, V_DIM:]
             - lam * (acc_scr[1, :, :V_DIM] / acc_scr[1, :, V_DIM:]))
        ms = jnp.mean(o * o, axis=-1, keepdims=True)
        o_ref[0] = o * lax.rsqrt(ms + EPS) * g_ref[...] * (1.0 - lam_init)


def _prompt_units(b, n_tiles, steps):
    units = np.array([(bi, h, i, kb) for bi in range(b) for h in range(N_HEADS)
                      for i in range(n_tiles) for kb in range(i + 1)], np.int32)
    n_units = len(units)
    assert n_units <= steps, "at most one prompt unit per grid step"
    done_before = (np.arange(steps + 1) * n_units) // steps
    has = (done_before[1:] > done_before[:-1]).astype(np.int32)
    per_step = units[np.minimum(done_before[:-1], n_units - 1)]
    return tuple(jnp.asarray(per_step[:, c]) for c in range(4)) + (jnp.asarray(has),)


def _attn_kernel(pt_ref, ub_ref, uh_ref, ui_ref, ukb_ref, uhas_ref, lam_ref, q_ref, knew_ref, vnew_ref, blast_ref,
                 bnew_ref, g_ref, ck_hbm, cv_hbm, pq_ref, pkt_ref, pv_ref, pbias_ref, o_ref, po_ref,
                 kbuf, vbuf, sem, qbd, knew_t, vpad, m_scr, l_scr, acc, vaug, pm_scr, pacc_scr,
                 *, gp, nbuf, lam_init, pt_tile, rc):
    del ub_ref, uh_ref
    gi = pl.program_id(1)
    n_groups = pl.num_programs(1)
    n_maps = 2 * N_HEADS
    t = q_ref.shape[1]
    d = q_ref.shape[2]
    rows = n_maps * t
    page = kbuf.shape[3]

    step = pl.program_id(0) * n_groups + gi
    total = pl.num_programs(0) * n_groups

    def page_copies(st, slot):
        sb = st // n_groups
        sg = st % n_groups
        out = []
        for jj in range(gp):
            pg = pt_ref[sb, sg * gp + jj]
            out.append(pltpu.make_async_copy(ck_hbm.at[pg], kbuf.at[slot, jj], sem.at[slot, 0, jj]))
            out.append(pltpu.make_async_copy(cv_hbm.at[pg], vbuf.at[slot, jj], sem.at[slot, 1, jj]))
        return out

    @pl.when(step == 0)
    def _():
        for st in range(nbuf - 1):
            for c in page_copies(st, st):
                c.start()

    ahead = step + (nbuf - 1)

    @pl.when(ahead < total)
    def _():
        for c in page_copies(ahead, ahead % nbuf):
            c.start()

    @pl.when(uhas_ref[step] == 1)
    def _():
        _prompt_unit(ukb_ref[step], ui_ref[step], lam_ref[0], pq_ref, pkt_ref, pv_ref, pbias_ref, g_ref, po_ref,
                     vaug, pm_scr, pacc_scr, t=pt_tile, rc=rc, lam_init=lam_init)

    slot = step % nbuf
    for c in page_copies(step, slot):
        c.wait()

    @pl.when(gi == 0)
    def _():
        qf = q_ref[0].astype(F32)
        qt = jnp.concatenate([qf] * n_maps, axis=0)
        rmap = lax.broadcasted_iota(jnp.int32, (rows, d), 0) // t
        cmap = lax.broadcasted_iota(jnp.int32, (rows, d), 1) // HEAD_DIM
        qbd[...] = jnp.where(rmap == cmap, qt, 0.0).astype(BF16)
        pad = jnp.zeros((page - t, d), F32)
        knew_t[...] = jnp.concatenate([knew_ref[0], pad], axis=0).T.astype(BF16)
        vpad[...] = jnp.concatenate([vnew_ref[0], pad], axis=0).astype(BF16)
        m_scr[...] = jnp.full(m_scr.shape, NEG_INF, F32)
        l_scr[...] = jnp.zeros(l_scr.shape, F32)
        acc[...] = jnp.zeros(acc.shape, F32)

    def update(kt, v_heads, bias):
        n = kt.shape[1]
        s = jnp.dot(qbd[...], kt, preferred_element_type=F32)
        if bias is not None:
            s = s + bias
        m_old = m_scr[...]
        m_new = jnp.maximum(m_old, jnp.max(s, axis=-1, keepdims=True))
        p = jnp.exp(s - _lane_tile(m_new, n // LANES))
        alpha = jnp.exp(m_old - m_new)
        l_scr[...] = alpha * l_scr[...] + jnp.sum(p, axis=-1, keepdims=True)
        m_scr[...] = m_new
        pb = p.astype(BF16)
        pv = jnp.concatenate([jnp.dot(pb[2 * t * h:2 * t * (h + 1), :], v_heads[h], preferred_element_type=F32)
                              for h in range(N_HEADS)], axis=0)
        acc[...] = alpha * acc[...] + pv

    def pages():
        kt = jnp.concatenate([kbuf[slot, jj].astype(BF16) for jj in range(gp)], axis=1)
        v_heads = [jnp.concatenate([vbuf[slot, jj, pl.ds(h, page, stride=N_HEADS), :].astype(BF16)
                                    for jj in range(gp)], axis=0)
                   for h in range(N_HEADS)]
        return kt, v_heads

    last = pl.num_programs(1) - 1

    @pl.when(gi < last)
    def _():
        kt, v_heads = pages()
        update(kt, v_heads, None)

    @pl.when(gi == last)
    def _():
        kt, v_heads = pages()
        update(kt, v_heads, blast_ref[...])
        update(knew_t[...], [vpad[:, h * V_DIM:(h + 1) * V_DIM] for h in range(N_HEADS)], bnew_ref[...])
        lam = lam_ref[0]
        on = acc[...] / l_scr[...]
        for h in range(N_HEADS):
            r0 = 2 * h * t
            o = on[r0:r0 + t, :] - lam * on[r0 + t:r0 + 2 * t, :]
            ms = jnp.mean(o * o, axis=-1, keepdims=True)
            o_ref[0, :, h * V_DIM:(h + 1) * V_DIM] = o * lax.rsqrt(ms + EPS) * g_ref[...] * (1.0 - lam_init)


def _attention(q, k_new, v_new, cache_kt, cache_v2, page_table, pq, pkt, pv, rel_table, lam, lam_init, subln_g,
               gp, pt_tile):
    b, t, d = q.shape
    bp, s_len, _ = pq.shape
    rc = min(pt_tile, ROW_CHUNK)
    assert 2 * rc + 1 >= _far_distance() + rc, "tiles two or more row chunks below the diagonal see a constant bias"
    pbias = jnp.stack([_toeplitz_bias(0, rc, rel_table), _toeplitz_bias(rc, rc, rel_table)], axis=1)
    page = cache_kt.shape[2]
    n_pages = page_table.shape[1]
    past = n_pages * page
    rows = 2 * N_HEADS * t
    assert rows == LANES and V_DIM == LANES and page == LANES and n_pages % gp == 0
    far = _far_distance()
    keys = gp * page
    assert past - ((n_pages - gp) * page - 1) >= far, "all page groups but the last must see a constant bias"
    bias_last = _query_rows_bias(keys, t, keys, rel_table)
    bias_new = _query_rows_bias(0, t, page, rel_table)

    n_groups = n_pages // gp
    unit_tables = _prompt_units(bp, s_len // pt_tile, b * n_groups)

    seq = lambda bi, gi, *_: (bi, 0, 0)
    fixed = lambda bi, gi, *_: (0, 0)

    def unit_map(pick):
        def index_map(bi, gi, pt, ub, uh, ui, ukb, uhas):
            step = bi * n_groups + gi
            return pick(ub[step], uh[step], ui[step])
        return index_map

    nbuf = SAMPLE_RING_SLOTS
    assert b * n_groups >= nbuf - 1
    kern = functools.partial(_attn_kernel, gp=gp, nbuf=nbuf, lam_init=lam_init, pt_tile=pt_tile, rc=rc)
    grid_spec = pltpu.PrefetchScalarGridSpec(
        num_scalar_prefetch=6,
        grid=(b, n_groups),
        in_specs=[pl.BlockSpec(memory_space=pltpu.SMEM),
                  pl.BlockSpec((1, t, d), seq),
                  pl.BlockSpec((1, t, d), seq),
                  pl.BlockSpec((1, t, d), seq),
                  pl.BlockSpec((rows, keys), fixed),
                  pl.BlockSpec((rows, page), fixed),
                  pl.BlockSpec((1, V_DIM), fixed),
                  pl.BlockSpec(memory_space=pl.ANY),
                  pl.BlockSpec(memory_space=pl.ANY),
                  pl.BlockSpec((1, pt_tile, V_DIM), unit_map(lambda ub, uh, ui: (ub, ui, uh))),
                  pl.BlockSpec((1, V_DIM, s_len), unit_map(lambda ub, uh, ui: (ub, uh, 0))),
                  pl.BlockSpec((1, s_len, V_DIM), unit_map(lambda ub, uh, ui: (ub, 0, uh))),
                  pl.BlockSpec((1, 2, rc, rc), unit_map(lambda ub, uh, ui: (uh, 0, 0, 0)))],
        out_specs=[pl.BlockSpec((1, t, d), seq),
                   pl.BlockSpec((1, pt_tile, V_DIM), unit_map(lambda ub, uh, ui: (ub, ui, uh)))],
        scratch_shapes=[pltpu.VMEM((nbuf, gp, d, page), F32),
                        pltpu.VMEM((nbuf, gp, page * N_HEADS, V_DIM), F32),
                        pltpu.SemaphoreType.DMA((nbuf, 2, gp)),
                        pltpu.VMEM((rows, d), BF16),
                        pltpu.VMEM((d, page), BF16),
                        pltpu.VMEM((page, d), BF16),
                        pltpu.VMEM((rows, LANES), F32),
                        pltpu.VMEM((rows, LANES), F32),
                        pltpu.VMEM((rows, V_DIM), F32),
                        pltpu.VMEM((s_len, 2 * V_DIM), BF16),
                        pltpu.VMEM((2, pt_tile, LANES), F32),
                        pltpu.VMEM((2, pt_tile, 2 * V_DIM), F32)])
    return pl.pallas_call(
        kern,
        grid_spec=grid_spec,
        out_shape=[jax.ShapeDtypeStruct((b, t, d), F32), jax.ShapeDtypeStruct((bp, s_len, d), F32)],
        compiler_params=_cparams(("arbitrary", "arbitrary")),
        name="attention",
    )(page_table, *unit_tables, lam.reshape(1), q, k_new, v_new, bias_last, bias_new, subln_g.reshape(1, V_DIM),
      cache_kt, cache_v2, pq, pkt, pv, pbias)


def _outproj_kernel(rec_ref, att_ref, gatt_ref, x_ref, gate_ref, w_ref, o_ref):
    merged = rec_ref[...] + jax.nn.sigmoid(gatt_ref[...]) * att_ref[...]
    y = jnp.dot(merged.astype(BF16), w_ref[...], preferred_element_type=F32)
    o_ref[...] = x_ref[...] + gate_ref[0] * y


def _out_projection(rec, att, gatt, x2, gate, w_out_bf, tm, mod_map):
    n, d = x2.shape
    tok = pl.BlockSpec((tm, d), lambda i: (i, 0))
    mod_block = (1,) + gate.shape[1:]
    return pl.pallas_call(
        _outproj_kernel,
        grid=(n // tm,),
        in_specs=[tok, tok, tok, tok,
                  pl.BlockSpec(mod_block, lambda i: mod_map(i, 0)),
                  pl.BlockSpec((d, d), lambda i: (0, 0))],
        out_specs=tok,
        out_shape=jax.ShapeDtypeStruct((n, d), F32),
        compiler_params=_cparams(("arbitrary",)),
        name="out_projection",
    )(rec, att, gatt, x2, gate, w_out_bf)


def _ffn_kernel(x_ref, sc_ref, sh_ref, gate_ref, g2_ref, gf_ref, wu_ref, wd_ref, o_ref, h_scr, acc_scr):
    f = pl.program_id(1)

    @pl.when(f == 0)
    def _():
        x = x_ref[...]
        ms = jnp.mean(x * x, axis=-1, keepdims=True)
        y = x * lax.rsqrt(ms + EPS) * g2_ref[...]
        h_scr[...] = (y * (1.0 + sc_ref[0]) + sh_ref[0]).astype(BF16)
        acc_scr[...] = jnp.zeros(acc_scr.shape, F32)

    up = jnp.dot(h_scr[...], wu_ref[...], preferred_element_type=F32)
    act = jnp.square(jnp.maximum(up, 0.0)).astype(BF16)
    acc_scr[...] += jnp.dot(act, wd_ref[...], preferred_element_type=F32)

    @pl.when(f == pl.num_programs(1) - 1)
    def _():
        x = x_ref[...] + gate_ref[0] * acc_scr[...]
        ms = jnp.mean(x * x, axis=-1, keepdims=True)
        o_ref[...] = x * lax.rsqrt(ms + EPS) * gf_ref[...]


def _ffn(x2, scale, shift, gate, norm2_g, final_g, w_up_bf, w_down_bf, tm, tf, mod_map):
    n, d = x2.shape
    dff = w_up_bf.shape[1]
    tok = pl.BlockSpec((tm, d), lambda i, f: (i, 0))
    mod_block = (1,) + scale.shape[1:]
    mod_spec = pl.BlockSpec(mod_block, mod_map)
    vec = pl.BlockSpec((1, d), lambda i, f: (0, 0))
    return pl.pallas_call(
        _ffn_kernel,
        grid=(n // tm, dff // tf),
        in_specs=[tok, mod_spec, mod_spec, mod_spec, vec, vec,
                  pl.BlockSpec((d, tf), lambda i, f: (0, f)),
                  pl.BlockSpec((tf, d), lambda i, f: (f, 0))],
        out_specs=tok,
        out_shape=jax.ShapeDtypeStruct((n, d), F32),
        scratch_shapes=[pltpu.VMEM((tm, d), BF16), pltpu.VMEM((tm, d), F32)],
        compiler_params=_cparams(("arbitrary", "arbitrary")),
        name="ffn",
    )(x2, scale, shift, gate, norm2_g.reshape(1, d), final_g.reshape(1, d), w_up_bf, w_down_bf)


def _pick(n, pref):
    t = min(n, pref)
    assert n % t == 0
    return t


def kernel(x_prompt, x_sample, cache_k, cache_v, state_conv, state_lru, page_table, c_prompt, c_sample, rel_bias_table, norm1_g, norm2_g, w_mod, b_mod, w_in, conv_w, conv_b, lru_wa, lru_ba, lru_wx, lru_bx, lru_lambda, lambda_q1, lambda_k1, lambda_q2, lambda_k2, subln_g, w_out, w_up, w_down, final_norm_g):
    depth = w_in.shape[0]
    assert depth == 1
    layer = 0
    lam_init = 0.8 - 0.6 * math.exp(-0.3 * layer)
    bp, s, d = x_prompt.shape
    bs, t, _ = x_sample.shape
    n_pool, page = cache_k.shape[1], cache_k.shape[2]

    n_seq = bp + bs
    n_pad = -n_seq % (2 * SUBLANES)
    c_all = jnp.concatenate([c_prompt, c_sample, jnp.zeros((n_pad, d), F32)], axis=0)
    mod = _modulation(c_all, w_mod[layer], b_mod[layer])
    mods_p = [mod[:bp, i * d:(i + 1) * d].reshape(bp, 1, d) for i in range(6)]

    w_seg = jnp.transpose(w_in[layer].astype(BF16).reshape(d, N_SEG, d), (1, 0, 2))
    w_seg_kt = w_seg.at[K_SEG].set(w_seg[K_SEG].T)
    w_out_bf = w_out[layer].astype(BF16)
    w_up_bf = w_up[layer].astype(BF16)
    w_down_bf = w_down[layer].astype(BF16)
    wa_bf = lru_wa[layer].astype(BF16)
    wx_bf = lru_wx[layer].astype(BF16)
    lam = (jnp.exp(jnp.sum(lambda_q1[layer] * lambda_k1[layer]))
           - jnp.exp(jnp.sum(lambda_q2[layer] * lambda_k2[layer])) + lam_init).astype(F32)

    n = bp * s
    tm = _pick(s, 512)
    tps = s // tm
    shift1, scale1, gate1, shift2, scale2, gate2 = mods_p
    mod_map = lambda i, j: (i // tps, 0, 0)
    x2 = x_prompt.reshape(n, d)
    xrec, yrec, q, ktf, ktb, vf, vb, grec, gatt = _in_projection(
        x2, scale1, shift1, norm1_g[layer], w_seg_kt, tm, mod_map, s, True)
    rec, conv_p, lru_p = _recurrent_branch(
        xrec.reshape(bp, s, d), yrec.reshape(bp, s, d), grec.reshape(bp, s, d),
        jnp.zeros((bp, CONV_WIDTH - 1, d), F32), jnp.zeros((bp, d), F32),
        conv_w[layer], conv_b[layer], wa_bf, lru_ba[layer], wx_bf, lru_bx[layer], lru_lambda[layer],
        1, _pick(s, 256), 0)
    k_prompt = jnp.transpose(ktf.reshape(1, bp, 2 * N_HEADS, HEAD_DIM, s), (0, 1, 4, 2, 3))
    v_prompt = vf.reshape(1, bp, s, N_HEADS, V_DIM)

    n_s = bs * t
    tm_s = _pick(n_s, 512)
    assert tm_s % t == 0
    past = page_table.shape[1] * page
    shift1_s, scale1_s, gate1_s, shift2_s, scale2_s, gate2_s = [
        jnp.repeat(mod[bp:bp + bs, i * d:(i + 1) * d], t, axis=0).reshape(n_s // tm_s, tm_s, d) for i in range(6)]
    mod_map_s = lambda i, j: (i, 0, 0)
    x2_s = x_sample.reshape(n_s, d)
    xrec_s, yrec_s, q_s, kf_s, vf_s, grec_s, gatt_s = _in_projection(
        x2_s, scale1_s, shift1_s, norm1_g[layer], w_seg, tm_s, mod_map_s, t, False)
    rec_s, conv_s, lru_s = _recurrent_branch(
        xrec_s.reshape(bs, t, d), yrec_s.reshape(bs, t, d), grec_s.reshape(bs, t, d), state_conv[layer],
        state_lru[layer], conv_w[layer], conv_b[layer], wa_bf, lru_ba[layer], wx_bf, lru_bx[layer],
        lru_lambda[layer], _pick(bs, 32), t, past)

    cache_kt = jnp.transpose(cache_k[layer], (0, 2, 3, 1)).reshape(n_pool, d, page)
    cache_v2 = cache_v[layer].reshape(n_pool, page * N_HEADS, V_DIM)
    att_s, att = _attention(q_s.reshape(bs, t, d), kf_s.reshape(bs, t, d), vf_s.reshape(bs, t, d), cache_kt, cache_v2,
                            page_table, q.reshape(bp, s, d), ktb, vb.reshape(bp, s, d),
                            rel_bias_table, lam, lam_init, subln_g[layer],
                            _pick(page_table.shape[1], 4), _pick(s, 512))

    x1 = _out_projection(rec.reshape(n, d), att.reshape(n, d), gatt, x2, gate1, w_out_bf, tm, mod_map)
    y_prompt = _ffn(x1, scale2, shift2, gate2, norm2_g[layer], final_norm_g, w_up_bf, w_down_bf, tm,
                    _pick(w_up_bf.shape[1], 1024), mod_map).reshape(bp, s, d)
    x1_s = _out_projection(rec_s.reshape(n_s, d), att_s.reshape(n_s, d), gatt_s, x2_s, gate1_s, w_out_bf, tm_s,
                           mod_map_s)
    y_sample = _ffn(x1_s, scale2_s, shift2_s, gate2_s, norm2_g[layer], final_norm_g, w_up_bf, w_down_bf, tm_s,
                    _pick(w_up_bf.shape[1], 1024), mod_map_s).reshape(bs, t, d)
    k_sample = kf_s.reshape(1, bs, t, 2 * N_HEADS, HEAD_DIM)
    v_sample = vf_s.reshape(1, bs, t, N_HEADS, V_DIM)

    return (y_prompt, y_sample, k_prompt, v_prompt,
            conv_p.reshape(1, bp, CONV_WIDTH - 1, d), lru_p.reshape(1, bp, d),
            k_sample, v_sample,
            conv_s.reshape(1, bs, CONV_WIDTH - 1, d), lru_s.reshape(1, bs, d))
```

```python
import functools
import math

import numpy as np
import jax
import jax.numpy as jnp
from jax import lax
from jax.experimental import pallas as pl
from jax.experimental.pallas import tpu as pltpu

F32 = jnp.float32
BF16 = jnp.bfloat16

N_HEADS = 8
HEAD_DIM = 64
V_DIM = 2 * HEAD_DIM
LRU_BLOCKS = 4
CONV_WIDTH = 4
LRU_C = 8.0
NUM_BUCKETS = 32
MAX_DISTANCE = 128
EPS = 1e-6
NEG_INF = -1e30
N_SEG = 7
K_SEG = 3
LANES = 128
SUBLANES = 8
VMEM_LIMIT = 56 * 1024 * 1024
ROW_CHUNK = 256
MXU_TILE = 256
SAMPLE_RING_SLOTS = 5


def _lane_tile(x, n):
    return x if n == 1 else jnp.concatenate([x] * n, axis=1)


def _cparams(sem):
    return pltpu.CompilerParams(dimension_semantics=sem, vmem_limit_bytes=VMEM_LIMIT)


def _mod_kernel(c_ref, w_ref, b_ref, o_ref):
    c = c_ref[...]
    s = (c * jax.nn.sigmoid(c)).astype(BF16)
    o_ref[...] = jnp.dot(s, w_ref[...].astype(BF16), preferred_element_type=F32) + b_ref[...]


def _modulation(c_all, w_mod, b_mod):
    m, d = c_all.shape
    n = w_mod.shape[1]
    tn = d
    return pl.pallas_call(
        _mod_kernel,
        grid=(n // tn,),
        in_specs=[pl.BlockSpec((m, d), lambda j: (0, 0)),
                  pl.BlockSpec((d, tn), lambda j: (0, j)),
                  pl.BlockSpec((1, tn), lambda j: (0, j))],
        out_specs=pl.BlockSpec((m, tn), lambda j: (0, j)),
        out_shape=jax.ShapeDtypeStruct((m, n), F32),
        compiler_params=_cparams(("arbitrary",)),
        name="modulation",
    )(c_all, w_mod, b_mod.reshape(1, n))


def _inproj_kernel(x_ref, sc_ref, sh_ref, g_ref, w_ref, *rest, keys_transposed):
    if keys_transposed:
        xrec_ref, yrec_ref, q_ref, kf_ref, kb_ref, vf_ref, vb_ref, grec_ref, gatt_ref, h_scr = rest
    else:
        xrec_ref, yrec_ref, q_ref, kf_ref, vf_ref, grec_ref, gatt_ref, h_scr = rest

    x = x_ref[...]
    ms = jnp.mean(x * x, axis=-1, keepdims=True)
    y = x * lax.rsqrt(ms + EPS) * g_ref[...]
    h_scr[...] = (y * (1.0 + sc_ref[0]) + sh_ref[0]).astype(BF16)

    def project(j):
        return jnp.dot(h_scr[...], w_ref[j], preferred_element_type=F32)

    xrec_ref[...] = project(0)
    yrec_ref[...] = project(1)
    q_ref[...] = (project(2) * (HEAD_DIM ** -0.5)).astype(BF16)
    if keys_transposed:
        zt = lax.dot_general(w_ref[K_SEG], h_scr[...], (((1,), (1,)), ((), ())), preferred_element_type=F32)
        kf_ref[0] = zt
        kb_ref[0] = zt.astype(BF16)
    else:
        kf_ref[...] = project(K_SEG)
    z = project(4)
    vf_ref[...] = z
    if keys_transposed:
        vb_ref[...] = z.astype(BF16)
    grec_ref[...] = project(5)
    gatt_ref[...] = project(6)


def _in_projection(x2, scale, shift, norm_g, w_seg, tm, mod_map, seq_len, keys_transposed):
    n, d = x2.shape
    mod_block = (1,) + scale.shape[1:]
    tok = lambda i, j: (i, 0)
    f32_out = jax.ShapeDtypeStruct((n, d), F32)
    bf_out = jax.ShapeDtypeStruct((n, d), BF16)
    out_spec = pl.BlockSpec((tm, d), tok)
    if keys_transposed:
        tps = seq_len // tm
        kt_spec = pl.BlockSpec((1, d, tm), lambda i, j: (i // tps, 0, i % tps))
        kt_f32 = jax.ShapeDtypeStruct((n // seq_len, d, seq_len), F32)
        kt_bf = jax.ShapeDtypeStruct((n // seq_len, d, seq_len), BF16)
        out_specs = [out_spec, out_spec, out_spec, kt_spec, kt_spec, out_spec, out_spec, out_spec, out_spec]
        out_shape = [f32_out, f32_out, bf_out, kt_f32, kt_bf, f32_out, bf_out, f32_out, f32_out]
    else:
        out_specs = [out_spec] * 7
        out_shape = [f32_out, f32_out, bf_out, f32_out, f32_out, f32_out, f32_out]
    return pl.pallas_call(
        functools.partial(_inproj_kernel, keys_transposed=keys_transposed),
        grid=(n // tm, 1),
        in_specs=[pl.BlockSpec((tm, d), tok),
                  pl.BlockSpec(mod_block, mod_map),
                  pl.BlockSpec(mod_block, mod_map),
                  pl.BlockSpec((1, d), lambda i, j: (0, 0)),
                  pl.BlockSpec((N_SEG, d, d), lambda i, j: (0, 0, 0), pipeline_mode=pl.Buffered(1))],
        out_specs=out_specs,
        out_shape=out_shape,
        scratch_shapes=[pltpu.VMEM((tm, d), BF16)],
        compiler_params=_cparams(("arbitrary", "arbitrary")),
        name="in_projection",
    )(x2, scale, shift, norm_g.reshape(1, d), w_seg)


def _rec_kernel(x_ref, y_ref, g_ref, cprev_ref, hprev_ref, cw_ref, cb_ref, wa_ref, ba_ref, wx_ref, bx_ref,
                lam_ref, o_ref, cnew_ref, hnew_ref, xpad, a_scr, u_scr, hcar, *, nb, t, pos0):
    d = x_ref.shape[-1]
    ti = pl.program_id(1)
    groups = t // SUBLANES
    r = nb * groups

    @pl.when(ti == 0)
    def _():
        xpad[:, SUBLANES - (CONV_WIDTH - 1):SUBLANES, :] = cprev_ref[...]
        hcar[...] = hprev_ref[...]

    @pl.when(ti > 0)
    def _():
        xpad[:, 0:SUBLANES, :] = xpad[:, t:t + SUBLANES, :]

    x = x_ref[...]
    xpad[:, SUBLANES:, :] = x
    xc = cb_ref[...] + cw_ref[CONV_WIDTH - 1:CONV_WIDTH, :] * x
    for jj in range(CONV_WIDTH - 1):
        off = SUBLANES - (CONV_WIDTH - 1) + jj
        xc = xc + cw_ref[jj:jj + 1, :] * xpad[:, off:off + t, :]
    cnew_ref[...] = xpad[:, t + SUBLANES - (CONV_WIDTH - 1):t + SUBLANES, :]

    xc2 = xc.reshape(nb * t, d)
    xcb = xc2.astype(BF16)
    bw = d // LRU_BLOCKS
    ga = jnp.concatenate([jnp.dot(xcb[:, n * bw:(n + 1) * bw], wa_ref[n], preferred_element_type=F32)
                          for n in range(LRU_BLOCKS)], axis=-1)
    gx = jnp.concatenate([jnp.dot(xcb[:, n * bw:(n + 1) * bw], wx_ref[n], preferred_element_type=F32)
                          for n in range(LRU_BLOCKS)], axis=-1)
    gate_r = jax.nn.sigmoid(ga + ba_ref[...])
    gate_i = jax.nn.sigmoid(gx + bx_ref[...])
    z = -lam_ref[...]
    softplus = jnp.maximum(z, 0.0) + jnp.log1p(jnp.exp(-jnp.abs(z)))
    log_a = (-LRU_C) * gate_r * softplus
    a = jnp.exp(log_a)
    mult = jnp.sqrt(jnp.tanh(-log_a) * (1.0 + a * a))
    pos = pos0 + ti * t + lax.broadcasted_iota(jnp.int32, (nb, t, d), 1).reshape(nb * t, d)
    mult = jnp.where(pos == 0, 1.0, mult)
    u = mult * gate_i * xc2

    a3 = a.reshape(r, SUBLANES, d)
    u3 = u.reshape(r, SUBLANES, d)
    row = lax.broadcasted_iota(jnp.int32, (r, SUBLANES, d), 1)
    for s in (1, 2, 4):
        a_sh = pltpu.roll(a3, s, axis=1)
        u_sh = pltpu.roll(u3, s, axis=1)
        ok = row >= s
        u3 = jnp.where(ok, a3 * u_sh + u3, u3)
        a3 = jnp.where(ok, a3 * a_sh, a3)

    if groups == 1:
        h3 = a3 * hcar[...] + u3
        hcar[...] = h3[:, SUBLANES - 1:SUBLANES, :]
        h2 = h3.reshape(nb * t, d)
    else:
        assert nb == 1
        a_scr[...] = a3
        u_scr[...] = u3

        def body(gi, hin):
            hg = a_scr[gi] * hin + u_scr[gi]
            u_scr[gi] = hg
            return hg[SUBLANES - 1:SUBLANES, :]

        hcar[0] = lax.fori_loop(0, groups, body, hcar[0])
        h2 = u_scr[...].reshape(nb * t, d)
    hnew_ref[...] = hcar[...]
    yv = y_ref[...].reshape(nb * t, d)
    gv = g_ref[...].reshape(nb * t, d)
    o_ref[...] = (jax.nn.sigmoid(gv) * (h2 * jax.nn.gelu(yv))).reshape(nb, t, d)


def _recurrent_branch(xrec, yrec, grec, conv_prev, h_prev, conv_w, conv_b, wa_bf, ba, wx_bf, bx, lam, nb, t, pos0):
    b, s, d = xrec.shape
    groups = t // SUBLANES
    r = nb * groups
    blk = pl.BlockSpec((nb, t, d), lambda bi, ti: (bi, ti, 0))
    vec = pl.BlockSpec((1, d), lambda bi, ti: (0, 0))
    wblk = pl.BlockSpec(wa_bf.shape, lambda bi, ti: (0, 0, 0))
    kern = functools.partial(_rec_kernel, nb=nb, t=t, pos0=pos0)
    return pl.pallas_call(
        kern,
        grid=(b // nb, s // t),
        in_specs=[blk, blk, blk,
                  pl.BlockSpec((nb, CONV_WIDTH - 1, d), lambda bi, ti: (bi, 0, 0)),
                  pl.BlockSpec((nb, 1, d), lambda bi, ti: (bi, 0, 0)),
                  pl.BlockSpec((CONV_WIDTH, d), lambda bi, ti: (0, 0)),
                  vec, wblk, vec, wblk, vec, vec],
        out_specs=[blk,
                   pl.BlockSpec((nb, CONV_WIDTH - 1, d), lambda bi, ti: (bi, 0, 0)),
                   pl.BlockSpec((nb, 1, d), lambda bi, ti: (bi, 0, 0))],
        out_shape=[jax.ShapeDtypeStruct((b, s, d), F32),
                   jax.ShapeDtypeStruct((b, CONV_WIDTH - 1, d), F32),
                   jax.ShapeDtypeStruct((b, 1, d), F32)],
        scratch_shapes=[pltpu.VMEM((nb, t + SUBLANES, d), F32),
                        pltpu.VMEM((r, SUBLANES, d), F32),
                        pltpu.VMEM((r, SUBLANES, d), F32),
                        pltpu.VMEM((nb, 1, d), F32)],
        compiler_params=_cparams(("arbitrary", "arbitrary")),
        name="recurrent_branch",
    )(xrec, yrec, grec, conv_prev, h_prev.reshape(b, 1, d), conv_w, conv_b.reshape(1, d), wa_bf,
      ba.reshape(1, d), wx_bf, bx.reshape(1, d), lam.reshape(1, d))


def _bucket_np(rel):
    n = np.maximum(rel, 0)
    max_exact = NUM_BUCKETS // 2
    ratio = np.log(np.maximum(n, max_exact).astype(np.float32) / np.float32(max_exact)) / np.float32(
        math.log(MAX_DISTANCE / max_exact))
    large = max_exact + (ratio * np.float32(NUM_BUCKETS - max_exact)).astype(np.int32)
    large = np.minimum(large, NUM_BUCKETS - 1)
    return np.where(n < max_exact, n, large).astype(np.int32)


def _far_distance():
    far = 1
    while not np.all(_bucket_np(np.arange(far, far + 4 * MAX_DISTANCE)) == NUM_BUCKETS - 1):
        far += 1
    return far


def _bias_of_distances(dist, table):
    vals = table[jnp.asarray(_bucket_np(dist))] - table[NUM_BUCKETS - 1]
    return jnp.where(jnp.asarray(dist >= 0)[:, None], vals, NEG_INF).T


def _toeplitz_bias(base, t, table):
    period = 2 * t
    idx = np.arange(period)
    col_minus_row = np.where(idx < t, idx, idx - period)
    dist = np.where(idx == t, -1, base - col_minus_row)
    vec = _bias_of_distances(dist, table)
    skew = jnp.tile(vec, (1, t))[:, :t * (period - 1)].reshape(-1, t, period - 1)
    return skew[:, :, :t]


def _query_rows_bias(offset, t, keys, table):
    u = np.arange(keys + t - 1)
    vec = _bias_of_distances(offset + (t - 1) - u, table)
    per_query = jnp.stack([vec[:, t - 1 - tq:t - 1 - tq + keys] for tq in range(t)], axis=1)
    return jnp.repeat(per_query, 2, axis=0).reshape(2 * N_HEADS * t, keys)


def _prompt_unit(kb, i, lam, q_ref, kt_ref, v_ref, bias_ref, g_ref, o_ref, vaug, m_scr, acc_scr, *, t, rc, lam_init):
    @pl.when((i == 0) & (kb == 0))
    def _():
        vaug[:, :V_DIM] = v_ref[0]
        vaug[:, V_DIM:] = jnp.ones((vaug.shape[0], V_DIM), BF16)

    @pl.when(kb == 0)
    def _():
        m_scr[...] = jnp.full(m_scr.shape, NEG_INF, F32)
        acc_scr[...] = jnp.zeros(acc_scr.shape, F32)

    q = q_ref[0]
    lane = lax.broadcasted_iota(jnp.int32, q.shape, 1)
    zero = jnp.zeros_like(q)
    qz = (jnp.where(lane < HEAD_DIM, q, zero), jnp.where(lane >= HEAD_DIM, q, zero))
    start = pl.multiple_of(kb * t, t)

    def add_bias(s, base, r0):
        tiles = []
        for c0 in range(0, s.shape[1], rc):
            tile = s[:, c0:c0 + rc]
            delta = base + r0 - c0
            if delta < 0:
                tile = jnp.full(tile.shape, NEG_INF, F32)
            elif delta < 2 * rc:
                tile = tile + bias_ref[0, delta // rc]
            tiles.append(tile)
        return jnp.concatenate(tiles, axis=1)

    def block(base):
        causal = base == 0
        kt = kt_ref[0, :, pl.ds(start, t)]
        va = vaug[pl.ds(start, t), :]
        chains = [(r0, mp) for r0 in range(0, t, rc) for mp in range(2)]

        def keys_of(r0):
            return min(t, -(-(r0 + rc) // MXU_TILE) * MXU_TILE) if causal else t

        def logits(chain):
            r0, mp = chain
            return jnp.dot(qz[mp][r0:r0 + rc], kt[:, :keys_of(r0)], preferred_element_type=F32)

        s_next = logits(chains[0])
        for n, (r0, mp) in enumerate(chains):
            rows = slice(r0, r0 + rc)
            nk = keys_of(r0)
            s = s_next
            if n + 1 < len(chains):
                s_next = logits(chains[n + 1])
            if base is not None:
                s = add_bias(s, base, r0)
            m_old = m_scr[mp, rows]
            m_new = jnp.maximum(m_old, jnp.max(s, axis=-1, keepdims=True))
            p = jnp.exp(s - _lane_tile(m_new, nk // LANES)).astype(BF16)
            alpha = jnp.exp(m_old - m_new)
            acc_scr[mp, rows] = (_lane_tile(alpha, 2) * acc_scr[mp, rows]
                                 + jnp.dot(p, va[:nk], preferred_element_type=F32))
            m_scr[mp, rows] = m_new

    @pl.when(kb < i - 1)
    def _():
        block(None)

    @pl.when(kb == i - 1)
    def _():
        block(t)

    @pl.when(kb == i)
    def _():
        block(0)
        o = (acc_scr[0, :, :V_DIM] / acc_scr[0, :, V_DIM:]
             - lam * (acc_scr[1, :, :V_DIM] / acc_scr[1, :, V_DIM:]))
        ms = jnp.mean(o * o, axis=-1, keepdims=True)
        o_ref[0] = o * lax.rsqrt(ms + EPS) * g_ref[...] * (1.0 - lam_init)


def _prompt_units(b, n_tiles, steps):
    units = np.array([(bi, h, i, kb) for bi in range(b) for h in range(N_HEADS)
                      for i in range(n_tiles) for kb in range(i + 1)], np.int32)
    n_units = len(units)
    assert n_units <= steps, "at most one prompt unit per grid step"
    done_before = (np.arange(steps + 1) * n_units) // steps
    has = (done_before[1:] > done_before[:-1]).astype(np.int32)
    per_step = units[np.minimum(done_before[:-1], n_units - 1)]
    return tuple(jnp.asarray(per_step[:, c]) for c in range(4)) + (jnp.asarray(has),)


def _attn_kernel(pt_ref, ub_ref, uh_ref, ui_ref, ukb_ref, uhas_ref, lam_ref, q_ref, knew_ref, vnew_ref, blast_ref,
                 bnew_ref, g_ref, ck_hbm, cv_hbm, pq_ref, pkt_ref, pv_ref, pbias_ref, o_ref, po_ref,
                 kbuf, vbuf, sem, qbd, knew_t, vpad, m_scr, l_scr, acc, p_scr, alpha_scr, vaug, pm_scr, pacc_scr,
                 *, gp, nbuf, lam_init, pt_tile, rc):
    del ub_ref, uh_ref
    gi = pl.program_id(1)
    n_groups = pl.num_programs(1)
    n_maps = 2 * N_HEADS
    t = q_ref.shape[1]
    d = q_ref.shape[2]
    rows = n_maps * t
    page = kbuf.shape[3]

    step = pl.program_id(0) * n_groups + gi
    total = pl.num_programs(0) * n_groups

    def page_copies(st, slot):
        sb = st // n_groups
        sg = st % n_groups
        out = []
        for jj in range(gp):
            pg = pt_ref[sb, sg * gp + jj]
            out.append(pltpu.make_async_copy(ck_hbm.at[pg], kbuf.at[slot, jj], sem.at[slot, 0, jj]))
            out.append(pltpu.make_async_copy(cv_hbm.at[pg], vbuf.at[slot, jj], sem.at[slot, 1, jj]))
        return out

    lookahead = nbuf - 2

    @pl.when(step == 0)
    def _():
        for st in range(lookahead):
            for c in page_copies(st, st):
                c.start()

    ahead = step + lookahead

    @pl.when(ahead < total)
    def _():
        for c in page_copies(ahead, ahead % nbuf):
            c.start()

    @pl.when(uhas_ref[step] == 1)
    def _():
        _prompt_unit(ukb_ref[step], ui_ref[step], lam_ref[0], pq_ref, pkt_ref, pv_ref, pbias_ref, g_ref, po_ref,
                     vaug, pm_scr, pacc_scr, t=pt_tile, rc=rc, lam_init=lam_init)

    slot = step % nbuf
    for c in page_copies(step, slot):
        c.wait()

    @pl.when(gi == 0)
    def _():
        qf = q_ref[0].astype(F32)
        qt = jnp.concatenate([qf] * n_maps, axis=0)
        rmap = lax.broadcasted_iota(jnp.int32, (rows, d), 0) // t
        cmap = lax.broadcasted_iota(jnp.int32, (rows, d), 1) // HEAD_DIM
        qbd[...] = jnp.where(rmap == cmap, qt, 0.0).astype(BF16)
        pad = jnp.zeros((page - t, d), F32)
        knew_t[...] = jnp.concatenate([knew_ref[0], pad], axis=0).T.astype(BF16)
        vpad[...] = jnp.concatenate([vnew_ref[0], pad], axis=0).astype(BF16)
        m_scr[...] = jnp.full(m_scr.shape, NEG_INF, F32)
        l_scr[...] = jnp.zeros(l_scr.shape, F32)
        acc[...] = jnp.zeros(acc.shape, F32)
        p_scr[...] = jnp.zeros(p_scr.shape, BF16)
        alpha_scr[...] = jnp.ones(alpha_scr.shape, F32)

    @pl.when(step == 0)
    def _():
        vbuf[nbuf - 1] = jnp.zeros(vbuf.shape[1:], F32)

    def logits(kt):
        return jnp.dot(qbd[...], kt, preferred_element_type=F32)

    def softmax_step(s, bias):
        if bias is not None:
            s = s + bias
        m_old = m_scr[...]
        m_new = jnp.maximum(m_old, jnp.max(s, axis=-1, keepdims=True))
        p = jnp.exp(s - _lane_tile(m_new, s.shape[1] // LANES))
        alpha = jnp.exp(m_old - m_new)
        l_scr[...] = alpha * l_scr[...] + jnp.sum(p, axis=-1, keepdims=True)
        m_scr[...] = m_new
        return p.astype(BF16), alpha

    def values_step(pb, alpha, v_heads):
        pv = jnp.concatenate([jnp.dot(pb[2 * t * h:2 * t * (h + 1), :], v_heads[h], preferred_element_type=F32)
                              for h in range(N_HEADS)], axis=0)
        acc[...] = alpha * acc[...] + pv

    def page_keys(sl):
        return jnp.concatenate([kbuf[sl, jj].astype(BF16) for jj in range(gp)], axis=1)

    def page_values(sl):
        return [jnp.concatenate([vbuf[sl, jj, pl.ds(h, page, stride=N_HEADS), :].astype(BF16) for jj in range(gp)],
                                axis=0)
                for h in range(N_HEADS)]

    def pipelined_step(bias):
        s = logits(page_keys(slot))
        values_step(p_scr[...], alpha_scr[...], page_values((step + nbuf - 1) % nbuf))
        return softmax_step(s, bias)

    last = pl.num_programs(1) - 1

    @pl.when(gi < last)
    def _():
        pb, alpha = pipelined_step(None)
        p_scr[...] = pb
        alpha_scr[...] = alpha

    @pl.when(gi == last)
    def _():
        pb, alpha = pipelined_step(blast_ref[...])
        values_step(pb, alpha, page_values(slot))
        pb, alpha = softmax_step(logits(knew_t[...]), bnew_ref[...])
        values_step(pb, alpha, [vpad[:, h * V_DIM:(h + 1) * V_DIM] for h in range(N_HEADS)])
        lam = lam_ref[0]
        on = acc[...] / l_scr[...]
        for h in range(N_HEADS):
            r0 = 2 * h * t
            o = on[r0:r0 + t, :] - lam * on[r0 + t:r0 + 2 * t, :]
            ms = jnp.mean(o * o, axis=-1, keepdims=True)
            o_ref[0, :, h * V_DIM:(h + 1) * V_DIM] = o * lax.rsqrt(ms + EPS) * g_ref[...] * (1.0 - lam_init)


def _attention(q, k_new, v_new, cache_kt, cache_v2, page_table, pq, pkt, pv, rel_table, lam, lam_init, subln_g,
               gp, pt_tile):
    b, t, d = q.shape
    bp, s_len, _ = pq.shape
    rc = min(pt_tile, ROW_CHUNK)
    assert 2 * rc + 1 >= _far_distance() + rc, "tiles two or more row chunks below the diagonal see a constant bias"
    pbias = jnp.stack([_toeplitz_bias(0, rc, rel_table), _toeplitz_bias(rc, rc, rel_table)], axis=1)
    page = cache_kt.shape[2]
    n_pages = page_table.shape[1]
    past = n_pages * page
    rows = 2 * N_HEADS * t
    assert rows == LANES and V_DIM == LANES and page == LANES and n_pages % gp == 0
    far = _far_distance()
    keys = gp * page
    assert past - ((n_pages - gp) * page - 1) >= far, "all page groups but the last must see a constant bias"
    bias_last = _query_rows_bias(keys, t, keys, rel_table)
    bias_new = _query_rows_bias(0, t, page, rel_table)

    n_groups = n_pages // gp
    unit_tables = _prompt_units(bp, s_len // pt_tile, b * n_groups)

    seq = lambda bi, gi, *_: (bi, 0, 0)
    fixed = lambda bi, gi, *_: (0, 0)

    def unit_map(pick):
        def index_map(bi, gi, pt, ub, uh, ui, ukb, uhas):
            step = bi * n_groups + gi
            return pick(ub[step], uh[step], ui[step])
        return index_map

    nbuf = SAMPLE_RING_SLOTS
    assert b * n_groups >= nbuf - 1
    kern = functools.partial(_attn_kernel, gp=gp, nbuf=nbuf, lam_init=lam_init, pt_tile=pt_tile, rc=rc)
    grid_spec = pltpu.PrefetchScalarGridSpec(
        num_scalar_prefetch=6,
        grid=(b, n_groups),
        in_specs=[pl.BlockSpec(memory_space=pltpu.SMEM),
                  pl.BlockSpec((1, t, d), seq),
                  pl.BlockSpec((1, t, d), seq),
                  pl.BlockSpec((1, t, d), seq),
                  pl.BlockSpec((rows, keys), fixed),
                  pl.BlockSpec((rows, page), fixed),
                  pl.BlockSpec((1, V_DIM), fixed),
                  pl.BlockSpec(memory_space=pl.ANY),
                  pl.BlockSpec(memory_space=pl.ANY),
                  pl.BlockSpec((1, pt_tile, V_DIM), unit_map(lambda ub, uh, ui: (ub, ui, uh))),
                  pl.BlockSpec((1, V_DIM, s_len), unit_map(lambda ub, uh, ui: (ub, uh, 0))),
                  pl.BlockSpec((1, s_len, V_DIM), unit_map(lambda ub, uh, ui: (ub, 0, uh))),
                  pl.BlockSpec((1, 2, rc, rc), unit_map(lambda ub, uh, ui: (uh, 0, 0, 0)))],
        out_specs=[pl.BlockSpec((1, t, d), seq),
                   pl.BlockSpec((1, pt_tile, V_DIM), unit_map(lambda ub, uh, ui: (ub, ui, uh)))],
        scratch_shapes=[pltpu.VMEM((nbuf, gp, d, page), F32),
                        pltpu.VMEM((nbuf, gp, page * N_HEADS, V_DIM), F32),
                        pltpu.SemaphoreType.DMA((nbuf, 2, gp)),
                        pltpu.VMEM((rows, d), BF16),
                        pltpu.VMEM((d, page), BF16),
                        pltpu.VMEM((page, d), BF16),
                        pltpu.VMEM((rows, LANES), F32),
                        pltpu.VMEM((rows, LANES), F32),
                        pltpu.VMEM((rows, V_DIM), F32),
                        pltpu.VMEM((rows, keys), BF16),
                        pltpu.VMEM((rows, LANES), F32),
                        pltpu.VMEM((s_len, 2 * V_DIM), BF16),
                        pltpu.VMEM((2, pt_tile, LANES), F32),
                        pltpu.VMEM((2, pt_tile, 2 * V_DIM), F32)])
    return pl.pallas_call(
        kern,
        grid_spec=grid_spec,
        out_shape=[jax.ShapeDtypeStruct((b, t, d), F32), jax.ShapeDtypeStruct((bp, s_len, d), F32)],
        compiler_params=_cparams(("arbitrary", "arbitrary")),
        name="attention",
    )(page_table, *unit_tables, lam.reshape(1), q, k_new, v_new, bias_last, bias_new, subln_g.reshape(1, V_DIM),
      cache_kt, cache_v2, pq, pkt, pv, pbias)


def _outproj_kernel(rec_ref, att_ref, gatt_ref, x_ref, gate_ref, w_ref, o_ref):
    merged = rec_ref[...] + jax.nn.sigmoid(gatt_ref[...]) * att_ref[...]
    y = jnp.dot(merged.astype(BF16), w_ref[...], preferred_element_type=F32)
    o_ref[...] = x_ref[...] + gate_ref[0] * y


def _out_projection(rec, att, gatt, x2, gate, w_out_bf, tm, mod_map):
    n, d = x2.shape
    tok = pl.BlockSpec((tm, d), lambda i: (i, 0))
    mod_block = (1,) + gate.shape[1:]
    return pl.pallas_call(
        _outproj_kernel,
        grid=(n // tm,),
        in_specs=[tok, tok, tok, tok,
                  pl.BlockSpec(mod_block, lambda i: mod_map(i, 0)),
                  pl.BlockSpec((d, d), lambda i: (0, 0))],
        out_specs=tok,
        out_shape=jax.ShapeDtypeStruct((n, d), F32),
        compiler_params=_cparams(("arbitrary",)),
        name="out_projection",
    )(rec, att, gatt, x2, gate, w_out_bf)


def _ffn_kernel(x_ref, sc_ref, sh_ref, gate_ref, g2_ref, gf_ref, wu_ref, wd_ref, o_ref, h_scr, acc_scr):
    f = pl.program_id(1)

    @pl.when(f == 0)
    def _():
        x = x_ref[...]
        ms = jnp.mean(x * x, axis=-1, keepdims=True)
        y = x * lax.rsqrt(ms + EPS) * g2_ref[...]
        h_scr[...] = (y * (1.0 + sc_ref[0]) + sh_ref[0]).astype(BF16)
        acc_scr[...] = jnp.zeros(acc_scr.shape, F32)

    up = jnp.dot(h_scr[...], wu_ref[...], preferred_element_type=F32)
    act = jnp.square(jnp.maximum(up, 0.0)).astype(BF16)
    acc_scr[...] += jnp.dot(act, wd_ref[...], preferred_element_type=F32)

    @pl.when(f == pl.num_programs(1) - 1)
    def _():
        x = x_ref[...] + gate_ref[0] * acc_scr[...]
        ms = jnp.mean(x * x, axis=-1, keepdims=True)
        o_ref[...] = x * lax.rsqrt(ms + EPS) * gf_ref[...]


def _ffn(x2, scale, shift, gate, norm2_g, final_g, w_up_bf, w_down_bf, tm, tf, mod_map):
    n, d = x2.shape
    dff = w_up_bf.shape[1]
    tok = pl.BlockSpec((tm, d), lambda i, f: (i, 0))
    mod_block = (1,) + scale.shape[1:]
    mod_spec = pl.BlockSpec(mod_block, mod_map)
    vec = pl.BlockSpec((1, d), lambda i, f: (0, 0))
    return pl.pallas_call(
        _ffn_kernel,
        grid=(n // tm, dff // tf),
        in_specs=[tok, mod_spec, mod_spec, mod_spec, vec, vec,
                  pl.BlockSpec((d, tf), lambda i, f: (0, f)),
                  pl.BlockSpec((tf, d), lambda i, f: (f, 0))],
        out_specs=tok,
        out_shape=jax.ShapeDtypeStruct((n, d), F32),
        scratch_shapes=[pltpu.VMEM((tm, d), BF16), pltpu.VMEM((tm, d), F32)],
        compiler_params=_cparams(("arbitrary", "arbitrary")),
        name="ffn",
    )(x2, scale, shift, gate, norm2_g.reshape(1, d), final_g.reshape(1, d), w_up_bf, w_down_bf)


def _pick(n, pref):
    t = min(n, pref)
    assert n % t == 0
    return t


def kernel(x_prompt, x_sample, cache_k, cache_v, state_conv, state_lru, page_table, c_prompt, c_sample, rel_bias_table, norm1_g, norm2_g, w_mod, b_mod, w_in, conv_w, conv_b, lru_wa, lru_ba, lru_wx, lru_bx, lru_lambda, lambda_q1, lambda_k1, lambda_q2, lambda_k2, subln_g, w_out, w_up, w_down, final_norm_g):
    depth = w_in.shape[0]
    assert depth == 1
    layer = 0
    lam_init = 0.8 - 0.6 * math.exp(-0.3 * layer)
    bp, s, d = x_prompt.shape
    bs, t, _ = x_sample.shape
    n_pool, page = cache_k.shape[1], cache_k.shape[2]

    n_seq = bp + bs
    n_pad = -n_seq % (2 * SUBLANES)
    c_all = jnp.concatenate([c_prompt, c_sample, jnp.zeros((n_pad, d), F32)], axis=0)
    mod = _modulation(c_all, w_mod[layer], b_mod[layer])
    mods_p = [mod[:bp, i * d:(i + 1) * d].reshape(bp, 1, d) for i in range(6)]

    w_seg = jnp.transpose(w_in[layer].astype(BF16).reshape(d, N_SEG, d), (1, 0, 2))
    w_seg_kt = w_seg.at[K_SEG].set(w_seg[K_SEG].T)
    w_out_bf = w_out[layer].astype(BF16)
    w_up_bf = w_up[layer].astype(BF16)
    w_down_bf = w_down[layer].astype(BF16)
    wa_bf = lru_wa[layer].astype(BF16)
    wx_bf = lru_wx[layer].astype(BF16)
    lam = (jnp.exp(jnp.sum(lambda_q1[layer] * lambda_k1[layer]))
           - jnp.exp(jnp.sum(lambda_q2[layer] * lambda_k2[layer])) + lam_init).astype(F32)

    n = bp * s
    tm = _pick(s, 512)
    tps = s // tm
    shift1, scale1, gate1, shift2, scale2, gate2 = mods_p
    mod_map = lambda i, j: (i // tps, 0, 0)
    x2 = x_prompt.reshape(n, d)
    xrec, yrec, q, ktf, ktb, vf, vb, grec, gatt = _in_projection(
        x2, scale1, shift1, norm1_g[layer], w_seg_kt, tm, mod_map, s, True)
    rec, conv_p, lru_p = _recurrent_branch(
        xrec.reshape(bp, s, d), yrec.reshape(bp, s, d), grec.reshape(bp, s, d),
        jnp.zeros((bp, CONV_WIDTH - 1, d), F32), jnp.zeros((bp, d), F32),
        conv_w[layer], conv_b[layer], wa_bf, lru_ba[layer], wx_bf, lru_bx[layer], lru_lambda[layer],
        1, _pick(s, 256), 0)
    k_prompt = jnp.transpose(ktf.reshape(1, bp, 2 * N_HEADS, HEAD_DIM, s), (0, 1, 4, 2, 3))
    v_prompt = vf.reshape(1, bp, s, N_HEADS, V_DIM)

    n_s = bs * t
    tm_s = _pick(n_s, 512)
    assert tm_s % t == 0
    past = page_table.shape[1] * page
    shift1_s, scale1_s, gate1_s, shift2_s, scale2_s, gate2_s = [
        jnp.repeat(mod[bp:bp + bs, i * d:(i + 1) * d], t, axis=0).reshape(n_s // tm_s, tm_s, d) for i in range(6)]
    mod_map_s = lambda i, j: (i, 0, 0)
    x2_s = x_sample.reshape(n_s, d)
    xrec_s, yrec_s, q_s, kf_s, vf_s, grec_s, gatt_s = _in_projection(
        x2_s, scale1_s, shift1_s, norm1_g[layer], w_seg, tm_s, mod_map_s, t, False)
    rec_s, conv_s, lru_s = _recurrent_branch(
        xrec_s.reshape(bs, t, d), yrec_s.reshape(bs, t, d), grec_s.reshape(bs, t, d), state_conv[layer],
        state_lru[layer], conv_w[layer], conv_b[layer], wa_bf, lru_ba[layer], wx_bf, lru_bx[layer],
        lru_lambda[layer], _pick(bs, 32), t, past)

    cache_kt = jnp.transpose(cache_k[layer], (0, 2, 3, 1)).reshape(n_pool, d, page)
    cache_v2 = cache_v[layer].reshape(n_pool, page * N_HEADS, V_DIM)
    att_s, att = _attention(q_s.reshape(bs, t, d), kf_s.reshape(bs, t, d), vf_s.reshape(bs, t, d), cache_kt, cache_v2,
                            page_table, q.reshape(bp, s, d), ktb, vb.reshape(bp, s, d),
                            rel_bias_table, lam, lam_init, subln_g[layer],
                            _pick(page_table.shape[1], 4), _pick(s, 512))

    x1 = _out_projection(rec.reshape(n, d), att.reshape(n, d), gatt, x2, gate1, w_out_bf, tm, mod_map)
    y_prompt = _ffn(x1, scale2, shift2, gate2, norm2_g[layer], final_norm_g, w_up_bf, w_down_bf, tm,
                    _pick(w_up_bf.shape[1], 1024), mod_map).reshape(bp, s, d)
    x1_s = _out_projection(rec_s.reshape(n_s, d), att_s.reshape(n_s, d), gatt_s, x2_s, gate1_s, w_out_bf, tm_s,
                           mod_map_s)
    y_sample = _ffn(x1_s, scale2_s, shift2_s, gate2_s, norm2_g[layer], final_norm_g, w_up_bf, w_down_bf, tm_s,
                    _pick(w_up_bf.shape[1], 1024), mod_map_s).reshape(bs, t, d)
    k_sample = kf_s.reshape(1, bs, t, 2 * N_HEADS, HEAD_DIM)
    v_sample = vf_s.reshape(1, bs, t, N_HEADS, V_DIM)

    return (y_prompt, y_sample, k_prompt, v_prompt,
            conv_p.reshape(1, bp, CONV_WIDTH - 1, d), lru_p.reshape(1, bp, d),
            k_sample, v_sample,
            conv_s.reshape(1, bs, CONV_WIDTH - 1, d), lru_s.reshape(1, bs, d))
```

```python
import functools
import math

import numpy as np
import jax
import jax.numpy as jnp
from jax import lax
from jax.experimental import pallas as pl
from jax.experimental.pallas import tpu as pltpu

F32 = jnp.float32
BF16 = jnp.bfloat16

N_HEADS = 8
HEAD_DIM = 64
V_DIM = 2 * HEAD_DIM
LRU_BLOCKS = 4
CONV_WIDTH = 4
LRU_C = 8.0
NUM_BUCKETS = 32
MAX_DISTANCE = 128
EPS = 1e-6
NEG_INF = -1e30
N_SEG = 7
K_SEG = 3
LANES = 128
SUBLANES = 8
VMEM_LIMIT = 56 * 1024 * 1024
ROW_CHUNK = 256
MXU_TILE = 256
SAMPLE_RING_SLOTS = 6


def _lane_tile(x, n):
    return x if n == 1 else jnp.concatenate([x] * n, axis=1)


def _cparams(sem):
    return pltpu.CompilerParams(dimension_semantics=sem, vmem_limit_bytes=VMEM_LIMIT)


def _mod_kernel(c_ref, w_ref, b_ref, o_ref):
    c = c_ref[...]
    s = (c * jax.nn.sigmoid(c)).astype(BF16)
    o_ref[...] = jnp.dot(s, w_ref[...].astype(BF16), preferred_element_type=F32) + b_ref[...]


def _modulation(c_all, w_mod, b_mod):
    m, d = c_all.shape
    n = w_mod.shape[1]
    tn = d
    return pl.pallas_call(
        _mod_kernel,
        grid=(n // tn,),
        in_specs=[pl.BlockSpec((m, d), lambda j: (0, 0)),
                  pl.BlockSpec((d, tn), lambda j: (0, j)),
                  pl.BlockSpec((1, tn), lambda j: (0, j))],
        out_specs=pl.BlockSpec((m, tn), lambda j: (0, j)),
        out_shape=jax.ShapeDtypeStruct((m, n), F32),
        compiler_params=_cparams(("arbitrary",)),
        name="modulation",
    )(c_all, w_mod, b_mod.reshape(1, n))


def _inproj_kernel(x_ref, sc_ref, sh_ref, g_ref, w_ref, wkt_ref, *rest, keys_transposed):
    if keys_transposed:
        xrec_ref, yrec_ref, q_ref, kf_ref, kb_ref, vf_ref, vb_ref, grec_ref, gatt_ref, h_scr = rest
    else:
        xrec_ref, yrec_ref, q_ref, kf_ref, vf_ref, grec_ref, gatt_ref, h_scr = rest
    d = x_ref.shape[-1]

    x = x_ref[...]
    ms = jnp.mean(x * x, axis=-1, keepdims=True)
    y = x * lax.rsqrt(ms + EPS) * g_ref[...]
    h_scr[...] = (y * (1.0 + sc_ref[0]) + sh_ref[0]).astype(BF16)

    def project(j):
        return jnp.dot(h_scr[...], w_ref[:, j * d:(j + 1) * d], preferred_element_type=F32)

    xrec_ref[...] = project(0)
    yrec_ref[...] = project(1)
    q_ref[...] = (project(2) * (HEAD_DIM ** -0.5)).astype(BF16)
    if keys_transposed:
        zt = lax.dot_general(wkt_ref[...], h_scr[...], (((1,), (1,)), ((), ())), preferred_element_type=F32)
        kf_ref[0] = zt
        kb_ref[0] = zt.astype(BF16)
    else:
        kf_ref[...] = project(K_SEG)
    z = project(4)
    vf_ref[...] = z
    if keys_transposed:
        vb_ref[...] = z.astype(BF16)
    grec_ref[...] = project(5)
    gatt_ref[...] = project(6)


def _in_projection(x2, scale, shift, norm_g, w_in_bf, w_kt_bf, tm, mod_map, seq_len, keys_transposed):
    n, d = x2.shape
    mod_block = (1,) + scale.shape[1:]
    tok = lambda i, j: (i, 0)
    f32_out = jax.ShapeDtypeStruct((n, d), F32)
    bf_out = jax.ShapeDtypeStruct((n, d), BF16)
    out_spec = pl.BlockSpec((tm, d), tok)
    if keys_transposed:
        tps = seq_len // tm
        kt_spec = pl.BlockSpec((1, d, tm), lambda i, j: (i // tps, 0, i % tps))
        kt_f32 = jax.ShapeDtypeStruct((n // seq_len, d, seq_len), F32)
        kt_bf = jax.ShapeDtypeStruct((n // seq_len, d, seq_len), BF16)
        out_specs = [out_spec, out_spec, out_spec, kt_spec, kt_spec, out_spec, out_spec, out_spec, out_spec]
        out_shape = [f32_out, f32_out, bf_out, kt_f32, kt_bf, f32_out, bf_out, f32_out, f32_out]
    else:
        out_specs = [out_spec] * 7
        out_shape = [f32_out, f32_out, bf_out, f32_out, f32_out, f32_out, f32_out]
    return pl.pallas_call(
        functools.partial(_inproj_kernel, keys_transposed=keys_transposed),
        grid=(n // tm, 1),
        in_specs=[pl.BlockSpec((tm, d), tok),
                  pl.BlockSpec(mod_block, mod_map),
                  pl.BlockSpec(mod_block, mod_map),
                  pl.BlockSpec((1, d), lambda i, j: (0, 0)),
                  pl.BlockSpec((d, N_SEG * d), lambda i, j: (0, 0), pipeline_mode=pl.Buffered(1)),
                  pl.BlockSpec((d, d), lambda i, j: (0, 0), pipeline_mode=pl.Buffered(1))],
        out_specs=out_specs,
        out_shape=out_shape,
        scratch_shapes=[pltpu.VMEM((tm, d), BF16)],
        compiler_params=_cparams(("arbitrary", "arbitrary")),
        name="in_projection",
    )(x2, scale, shift, norm_g.reshape(1, d), w_in_bf, w_kt_bf)


def _rec_kernel(x_ref, y_ref, g_ref, cprev_ref, hprev_ref, cw_ref, cb_ref, wa_ref, ba_ref, wx_ref, bx_ref,
                lam_ref, o_ref, cnew_ref, hnew_ref, xpad, a_scr, u_scr, hcar, *, nb, t, pos0):
    d = x_ref.shape[-1]
    ti = pl.program_id(1)
    groups = t // SUBLANES
    r = nb * groups

    @pl.when(ti == 0)
    def _():
        xpad[:, SUBLANES - (CONV_WIDTH - 1):SUBLANES, :] = cprev_ref[...]
        hcar[...] = hprev_ref[...]

    @pl.when(ti > 0)
    def _():
        xpad[:, 0:SUBLANES, :] = xpad[:, t:t + SUBLANES, :]

    x = x_ref[...]
    xpad[:, SUBLANES:, :] = x
    xc = cb_ref[...] + cw_ref[CONV_WIDTH - 1:CONV_WIDTH, :] * x
    for jj in range(CONV_WIDTH - 1):
        off = SUBLANES - (CONV_WIDTH - 1) + jj
        xc = xc + cw_ref[jj:jj + 1, :] * xpad[:, off:off + t, :]
    cnew_ref[...] = xpad[:, t + SUBLANES - (CONV_WIDTH - 1):t + SUBLANES, :]

    xc2 = xc.reshape(nb * t, d)
    xcb = xc2.astype(BF16)
    bw = d // LRU_BLOCKS
    ga = jnp.concatenate([jnp.dot(xcb[:, n * bw:(n + 1) * bw], wa_ref[n], preferred_element_type=F32)
                          for n in range(LRU_BLOCKS)], axis=-1)
    gx = jnp.concatenate([jnp.dot(xcb[:, n * bw:(n + 1) * bw], wx_ref[n], preferred_element_type=F32)
                          for n in range(LRU_BLOCKS)], axis=-1)
    gate_r = jax.nn.sigmoid(ga + ba_ref[...])
    gate_i = jax.nn.sigmoid(gx + bx_ref[...])
    z = -lam_ref[...]
    softplus = jnp.maximum(z, 0.0) + jnp.log1p(jnp.exp(-jnp.abs(z)))
    log_a = (-LRU_C) * gate_r * softplus
    a = jnp.exp(log_a)
    mult = jnp.sqrt(jnp.tanh(-log_a) * (1.0 + a * a))
    pos = pos0 + ti * t + lax.broadcasted_iota(jnp.int32, (nb, t, d), 1).reshape(nb * t, d)
    mult = jnp.where(pos == 0, 1.0, mult)
    u = mult * gate_i * xc2

    a3 = a.reshape(r, SUBLANES, d)
    u3 = u.reshape(r, SUBLANES, d)
    row = lax.broadcasted_iota(jnp.int32, (r, SUBLANES, d), 1)
    for s in (1, 2, 4):
        a_sh = pltpu.roll(a3, s, axis=1)
        u_sh = pltpu.roll(u3, s, axis=1)
        ok = row >= s
        u3 = jnp.where(ok, a3 * u_sh + u3, u3)
        a3 = jnp.where(ok, a3 * a_sh, a3)

    if groups == 1:
        h3 = a3 * hcar[...] + u3
        hcar[...] = h3[:, SUBLANES - 1:SUBLANES, :]
        h2 = h3.reshape(nb * t, d)
    else:
        assert nb == 1
        a_scr[...] = a3
        u_scr[...] = u3

        def body(gi, hin):
            hg = a_scr[gi] * hin + u_scr[gi]
            u_scr[gi] = hg
            return hg[SUBLANES - 1:SUBLANES, :]

        hcar[0] = lax.fori_loop(0, groups, body, hcar[0])
        h2 = u_scr[...].reshape(nb * t, d)
    hnew_ref[...] = hcar[...]
    yv = y_ref[...].reshape(nb * t, d)
    gv = g_ref[...].reshape(nb * t, d)
    o_ref[...] = (jax.nn.sigmoid(gv) * (h2 * jax.nn.gelu(yv))).reshape(nb, t, d)


def _recurrent_branch(xrec, yrec, grec, conv_prev, h_prev, conv_w, conv_b, wa_bf, ba, wx_bf, bx, lam, nb, t, pos0):
    b, s, d = xrec.shape
    groups = t // SUBLANES
    r = nb * groups
    blk = pl.BlockSpec((nb, t, d), lambda bi, ti: (bi, ti, 0))
    vec = pl.BlockSpec((1, d), lambda bi, ti: (0, 0))
    wblk = pl.BlockSpec(wa_bf.shape, lambda bi, ti: (0, 0, 0))
    kern = functools.partial(_rec_kernel, nb=nb, t=t, pos0=pos0)
    return pl.pallas_call(
        kern,
        grid=(b // nb, s // t),
        in_specs=[blk, blk, blk,
                  pl.BlockSpec((nb, CONV_WIDTH - 1, d), lambda bi, ti: (bi, 0, 0)),
                  pl.BlockSpec((nb, 1, d), lambda bi, ti: (bi, 0, 0)),
                  pl.BlockSpec((CONV_WIDTH, d), lambda bi, ti: (0, 0)),
                  vec, wblk, vec, wblk, vec, vec],
        out_specs=[blk,
                   pl.BlockSpec((nb, CONV_WIDTH - 1, d), lambda bi, ti: (bi, 0, 0)),
                   pl.BlockSpec((nb, 1, d), lambda bi, ti: (bi, 0, 0))],
        out_shape=[jax.ShapeDtypeStruct((b, s, d), F32),
                   jax.ShapeDtypeStruct((b, CONV_WIDTH - 1, d), F32),
                   jax.ShapeDtypeStruct((b, 1, d), F32)],
        scratch_shapes=[pltpu.VMEM((nb, t + SUBLANES, d), F32),
                        pltpu.VMEM((r, SUBLANES, d), F32),
                        pltpu.VMEM((r, SUBLANES, d), F32),
                        pltpu.VMEM((nb, 1, d), F32)],
        compiler_params=_cparams(("arbitrary", "arbitrary")),
        name="recurrent_branch",
    )(xrec, yrec, grec, conv_prev, h_prev.reshape(b, 1, d), conv_w, conv_b.reshape(1, d), wa_bf,
      ba.reshape(1, d), wx_bf, bx.reshape(1, d), lam.reshape(1, d))


def _bucket_np(rel):
    n = np.maximum(rel, 0)
    max_exact = NUM_BUCKETS // 2
    ratio = np.log(np.maximum(n, max_exact).astype(np.float32) / np.float32(max_exact)) / np.float32(
        math.log(MAX_DISTANCE / max_exact))
    large = max_exact + (ratio * np.float32(NUM_BUCKETS - max_exact)).astype(np.int32)
    large = np.minimum(large, NUM_BUCKETS - 1)
    return np.where(n < max_exact, n, large).astype(np.int32)


def _far_distance():
    far = 1
    while not np.all(_bucket_np(np.arange(far, far + 4 * MAX_DISTANCE)) == NUM_BUCKETS - 1):
        far += 1
    return far


def _bias_of_distances(dist, table):
    vals = table[jnp.asarray(_bucket_np(dist))] - table[NUM_BUCKETS - 1]
    return jnp.where(jnp.asarray(dist >= 0)[:, None], vals, NEG_INF).T


def _toeplitz_bias(base, t, table):
    period = 2 * t
    idx = np.arange(period)
    col_minus_row = np.where(idx < t, idx, idx - period)
    dist = np.where(idx == t, -1, base - col_minus_row)
    vec = _bias_of_distances(dist, table)
    skew = jnp.tile(vec, (1, t))[:, :t * (period - 1)].reshape(-1, t, period - 1)
    return skew[:, :, :t]


def _query_rows_bias(offset, t, keys, table):
    u = np.arange(keys + t - 1)
    vec = _bias_of_distances(offset + (t - 1) - u, table)
    per_query = jnp.stack([vec[:, t - 1 - tq:t - 1 - tq + keys] for tq in range(t)], axis=1)
    return jnp.repeat(per_query, 2, axis=0).reshape(2 * N_HEADS * t, keys)


def _prompt_unit(kb, i, lam, q_ref, kt_ref, v_ref, bias_ref, g_ref, o_ref, vaug, m_scr, acc_scr, *, t, rc, lam_init):
    @pl.when((i == 0) & (kb == 0))
    def _():
        vaug[:, :V_DIM] = v_ref[0]
        vaug[:, V_DIM:] = jnp.ones((vaug.shape[0], V_DIM), BF16)

    @pl.when(kb == 0)
    def _():
        m_scr[...] = jnp.full(m_scr.shape, NEG_INF, F32)
        acc_scr[...] = jnp.zeros(acc_scr.shape, F32)

    q = q_ref[0]
    lane = lax.broadcasted_iota(jnp.int32, q.shape, 1)
    zero = jnp.zeros_like(q)
    qz = (jnp.where(lane < HEAD_DIM, q, zero), jnp.where(lane >= HEAD_DIM, q, zero))
    start = pl.multiple_of(kb * t, t)

    def add_bias(s, base, r0):
        tiles = []
        for c0 in range(0, s.shape[1], rc):
            tile = s[:, c0:c0 + rc]
            delta = base + r0 - c0
            if delta < 0:
                tile = jnp.full(tile.shape, NEG_INF, F32)
            elif delta < 2 * rc:
                tile = tile + bias_ref[0, delta // rc]
            tiles.append(tile)
        return jnp.concatenate(tiles, axis=1)

    def block(base):
        causal = base == 0
        kt = kt_ref[0, :, pl.ds(start, t)]
        va = vaug[pl.ds(start, t), :]
        chains = [(r0, mp) for r0 in range(0, t, rc) for mp in range(2)]

        def keys_of(r0):
            return min(t, -(-(r0 + rc) // MXU_TILE) * MXU_TILE) if causal else t

        def logits(chain):
            r0, mp = chain
            return jnp.dot(qz[mp][r0:r0 + rc], kt[:, :keys_of(r0)], preferred_element_type=F32)

        s_next = logits(chains[0])
        for n, (r0, mp) in enumerate(chains):
            rows = slice(r0, r0 + rc)
            nk = keys_of(r0)
            s = s_next
            if n + 1 < len(chains):
                s_next = logits(chains[n + 1])
            if base is not None:
                s = add_bias(s, base, r0)
            m_old = m_scr[mp, rows]
            m_new = jnp.maximum(m_old, jnp.max(s, axis=-1, keepdims=True))
            p = jnp.exp(s - _lane_tile(m_new, nk // LANES)).astype(BF16)
            alpha = jnp.exp(m_old - m_new)
            acc_scr[mp, rows] = (_lane_tile(alpha, 2) * acc_scr[mp, rows]
                                 + jnp.dot(p, va[:nk], preferred_element_type=F32))
            m_scr[mp, rows] = m_new

    @pl.when(kb < i - 1)
    def _():
        block(None)

    @pl.when(kb == i - 1)
    def _():
        block(t)

    @pl.when(kb == i)
    def _():
        block(0)
        o = (acc_scr[0, :, :V_DIM] / acc_scr[0, :, V_DIM:]
             - lam * (acc_scr[1, :, :V_DIM] / acc_scr[1, :, V_DIM:]))
        ms = jnp.mean(o * o, axis=-1, keepdims=True)
        o_ref[0] = o * lax.rsqrt(ms + EPS) * g_ref[...] * (1.0 - lam_init)


def _prompt_units(b, n_tiles, steps):
    units = np.array([(bi, h, i, kb) for bi in range(b) for h in range(N_HEADS)
                      for i in range(n_tiles) for kb in range(i + 1)], np.int32)
    n_units = len(units)
    assert n_units <= steps, "at most one prompt unit per grid step"
    done_before = (np.arange(steps + 1) * n_units) // steps
    has = (done_before[1:] > done_before[:-1]).astype(np.int32)
    per_step = units[np.minimum(done_before[:-1], n_units - 1)]
    return tuple(jnp.asarray(per_step[:, c]) for c in range(4)) + (jnp.asarray(has),)


def _attn_kernel(pt_ref, ub_ref, uh_ref, ui_ref, ukb_ref, uhas_ref, lam_ref, q_ref, knew_ref, vnew_ref, blast_ref,
                 bnew_ref, g_ref, ck_hbm, cv_hbm, pq_ref, pkt_ref, pv_ref, pbias_ref, o_ref, po_ref,
                 kbuf, vbuf, sem, qbd, knew_t, vpad, m_scr, l_scr, acc, p_scr, alpha_scr, vaug, pm_scr, pacc_scr,
                 *, gp, nbuf, lam_init, pt_tile, rc):
    del ub_ref, uh_ref
    gi = pl.program_id(1)
    n_groups = pl.num_programs(1)
    n_maps = 2 * N_HEADS
    t = q_ref.shape[1]
    d = q_ref.shape[2]
    rows = n_maps * t
    page = kbuf.shape[3]

    step = pl.program_id(0) * n_groups + gi
    total = pl.num_programs(0) * n_groups

    def page_copies(st, slot):
        sb = st // n_groups
        sg = st % n_groups
        out = []
        for jj in range(gp):
            pg = pt_ref[sb, sg * gp + jj]
            out.append(pltpu.make_async_copy(ck_hbm.at[pg], kbuf.at[slot, jj], sem.at[slot, 0, jj]))
            out.append(pltpu.make_async_copy(cv_hbm.at[pg], vbuf.at[slot, jj], sem.at[slot, 1, jj]))
        return out

    lookahead = nbuf - 2

    @pl.when(step == 0)
    def _():
        for st in range(lookahead):
            for c in page_copies(st, st):
                c.start()

    ahead = step + lookahead

    @pl.when(ahead < total)
    def _():
        for c in page_copies(ahead, ahead % nbuf):
            c.start()

    @pl.when(uhas_ref[step] == 1)
    def _():
        _prompt_unit(ukb_ref[step], ui_ref[step], lam_ref[0], pq_ref, pkt_ref, pv_ref, pbias_ref, g_ref, po_ref,
                     vaug, pm_scr, pacc_scr, t=pt_tile, rc=rc, lam_init=lam_init)

    slot = step % nbuf
    for c in page_copies(step, slot):
        c.wait()

    @pl.when(gi == 0)
    def _():
        qf = q_ref[0].astype(F32)
        qt = jnp.concatenate([qf] * n_maps, axis=0)
        rmap = lax.broadcasted_iota(jnp.int32, (rows, d), 0) // t
        cmap = lax.broadcasted_iota(jnp.int32, (rows, d), 1) // HEAD_DIM
        qbd[...] = jnp.where(rmap == cmap, qt, 0.0).astype(BF16)
        pad = jnp.zeros((page - t, d), F32)
        knew_t[...] = jnp.concatenate([knew_ref[0], pad], axis=0).T.astype(BF16)
        vpad[...] = jnp.concatenate([vnew_ref[0], pad], axis=0).astype(BF16)
        m_scr[...] = jnp.full(m_scr.shape, NEG_INF, F32)
        l_scr[...] = jnp.zeros(l_scr.shape, F32)
        acc[...] = jnp.zeros(acc.shape, F32)
        p_scr[...] = jnp.zeros(p_scr.shape, BF16)
        alpha_scr[...] = jnp.ones(alpha_scr.shape, F32)

    @pl.when(step == 0)
    def _():
        vbuf[nbuf - 1] = jnp.zeros(vbuf.shape[1:], F32)

    def logits(kt):
        return jnp.dot(qbd[...], kt, preferred_element_type=F32)

    def softmax_step(s, bias):
        if bias is not None:
            s = s + bias
        m_old = m_scr[...]
        m_new = jnp.maximum(m_old, jnp.max(s, axis=-1, keepdims=True))
        p = jnp.exp(s - _lane_tile(m_new, s.shape[1] // LANES))
        alpha = jnp.exp(m_old - m_new)
        l_scr[...] = alpha * l_scr[...] + jnp.sum(p, axis=-1, keepdims=True)
        m_scr[...] = m_new
        return p.astype(BF16), alpha

    def values_step(pb, alpha, v_heads):
        pv = jnp.concatenate([jnp.dot(pb[2 * t * h:2 * t * (h + 1), :], v_heads[h], preferred_element_type=F32)
                              for h in range(N_HEADS)], axis=0)
        acc[...] = alpha * acc[...] + pv

    def page_keys(sl):
        return jnp.concatenate([kbuf[sl, jj].astype(BF16) for jj in range(gp)], axis=1)

    def page_values(sl):
        return [jnp.concatenate([vbuf[sl, jj, pl.ds(h, page, stride=N_HEADS), :].astype(BF16) for jj in range(gp)],
                                axis=0)
                for h in range(N_HEADS)]

    def pipelined_step(bias):
        s = logits(page_keys(slot))
        values_step(p_scr[...], alpha_scr[...], page_values((step + nbuf - 1) % nbuf))
        return softmax_step(s, bias)

    last = pl.num_programs(1) - 1

    @pl.when(gi < last)
    def _():
        pb, alpha = pipelined_step(None)
        p_scr[...] = pb
        alpha_scr[...] = alpha

    @pl.when(gi == last)
    def _():
        pb, alpha = pipelined_step(blast_ref[...])
        values_step(pb, alpha, page_values(slot))
        pb, alpha = softmax_step(logits(knew_t[...]), bnew_ref[...])
        values_step(pb, alpha, [vpad[:, h * V_DIM:(h + 1) * V_DIM] for h in range(N_HEADS)])
        lam = lam_ref[0]
        on = acc[...] / l_scr[...]
        for h in range(N_HEADS):
            r0 = 2 * h * t
            o = on[r0:r0 + t, :] - lam * on[r0 + t:r0 + 2 * t, :]
            ms = jnp.mean(o * o, axis=-1, keepdims=True)
            o_ref[0, :, h * V_DIM:(h + 1) * V_DIM] = o * lax.rsqrt(ms + EPS) * g_ref[...] * (1.0 - lam_init)


def _attention(q, k_new, v_new, cache_kt, cache_v2, page_table, pq, pkt, pv, rel_table, lam, lam_init, subln_g,
               gp, pt_tile):
    b, t, d = q.shape
    bp, s_len, _ = pq.shape
    rc = min(pt_tile, ROW_CHUNK)
    assert 2 * rc + 1 >= _far_distance() + rc, "tiles two or more row chunks below the diagonal see a constant bias"
    pbias = jnp.stack([_toeplitz_bias(0, rc, rel_table), _toeplitz_bias(rc, rc, rel_table)], axis=1)
    page = cache_kt.shape[2]
    n_pages = page_table.shape[1]
    past = n_pages * page
    rows = 2 * N_HEADS * t
    assert rows == LANES and V_DIM == LANES and page == LANES and n_pages % gp == 0
    far = _far_distance()
    keys = gp * page
    assert past - ((n_pages - gp) * page - 1) >= far, "all page groups but the last must see a constant bias"
    bias_last = _query_rows_bias(keys, t, keys, rel_table)
    bias_new = _query_rows_bias(0, t, page, rel_table)

    n_groups = n_pages // gp
    unit_tables = _prompt_units(bp, s_len // pt_tile, b * n_groups)

    seq = lambda bi, gi, *_: (bi, 0, 0)
    fixed = lambda bi, gi, *_: (0, 0)

    def unit_map(pick):
        def index_map(bi, gi, pt, ub, uh, ui, ukb, uhas):
            step = bi * n_groups + gi
            return pick(ub[step], uh[step], ui[step])
        return index_map

    nbuf = SAMPLE_RING_SLOTS
    assert b * n_groups >= nbuf - 1
    kern = functools.partial(_attn_kernel, gp=gp, nbuf=nbuf, lam_init=lam_init, pt_tile=pt_tile, rc=rc)
    grid_spec = pltpu.PrefetchScalarGridSpec(
        num_scalar_prefetch=6,
        grid=(b, n_groups),
        in_specs=[pl.BlockSpec(memory_space=pltpu.SMEM),
                  pl.BlockSpec((1, t, d), seq),
                  pl.BlockSpec((1, t, d), seq),
                  pl.BlockSpec((1, t, d), seq),
                  pl.BlockSpec((rows, keys), fixed),
                  pl.BlockSpec((rows, page), fixed),
                  pl.BlockSpec((1, V_DIM), fixed),
                  pl.BlockSpec(memory_space=pl.ANY),
                  pl.BlockSpec(memory_space=pl.ANY),
                  pl.BlockSpec((1, pt_tile, V_DIM), unit_map(lambda ub, uh, ui: (ub, ui, uh))),
                  pl.BlockSpec((1, V_DIM, s_len), unit_map(lambda ub, uh, ui: (ub, uh, 0))),
                  pl.BlockSpec((1, s_len, V_DIM), unit_map(lambda ub, uh, ui: (ub, 0, uh))),
                  pl.BlockSpec((1, 2, rc, rc), unit_map(lambda ub, uh, ui: (uh, 0, 0, 0)))],
        out_specs=[pl.BlockSpec((1, t, d), seq),
                   pl.BlockSpec((1, pt_tile, V_DIM), unit_map(lambda ub, uh, ui: (ub, ui, uh)))],
        scratch_shapes=[pltpu.VMEM((nbuf, gp, d, page), F32),
                        pltpu.VMEM((nbuf, gp, page * N_HEADS, V_DIM), F32),
                        pltpu.SemaphoreType.DMA((nbuf, 2, gp)),
                        pltpu.VMEM((rows, d), BF16),
                        pltpu.VMEM((d, page), BF16),
                        pltpu.VMEM((page, d), BF16),
                        pltpu.VMEM((rows, LANES), F32),
                        pltpu.VMEM((rows, LANES), F32),
                        pltpu.VMEM((rows, V_DIM), F32),
                        pltpu.VMEM((rows, keys), BF16),
                        pltpu.VMEM((rows, LANES), F32),
                        pltpu.VMEM((s_len, 2 * V_DIM), BF16),
                        pltpu.VMEM((2, pt_tile, LANES), F32),
                        pltpu.VMEM((2, pt_tile, 2 * V_DIM), F32)])
    return pl.pallas_call(
        kern,
        grid_spec=grid_spec,
        out_shape=[jax.ShapeDtypeStruct((b, t, d), F32), jax.ShapeDtypeStruct((bp, s_len, d), F32)],
        compiler_params=_cparams(("arbitrary", "arbitrary")),
        name="attention",
    )(page_table, *unit_tables, lam.reshape(1), q, k_new, v_new, bias_last, bias_new, subln_g.reshape(1, V_DIM),
      cache_kt, cache_v2, pq, pkt, pv, pbias)


def _tail_kernel(rec_ref, att_ref, gatt_ref, x_ref, gate1_ref, sc_ref, sh_ref, gate2_ref, g2_ref, gf_ref,
                 wo_ref, wu_ref, wd_ref, o_ref, x1_scr, h_scr, acc_scr):
    f = pl.program_id(1)

    @pl.when(f == 0)
    def _():
        merged = rec_ref[...] + jax.nn.sigmoid(gatt_ref[...]) * att_ref[...]
        y = jnp.dot(merged.astype(BF16), wo_ref[...], preferred_element_type=F32)
        x = x_ref[...] + gate1_ref[0] * y
        x1_scr[...] = x
        ms = jnp.mean(x * x, axis=-1, keepdims=True)
        h = x * lax.rsqrt(ms + EPS) * g2_ref[...]
        h_scr[...] = (h * (1.0 + sc_ref[0]) + sh_ref[0]).astype(BF16)
        acc_scr[...] = jnp.zeros(acc_scr.shape, F32)

    up = jnp.dot(h_scr[...], wu_ref[...], preferred_element_type=F32)
    act = jnp.square(jnp.maximum(up, 0.0)).astype(BF16)
    acc_scr[...] += jnp.dot(act, wd_ref[...], preferred_element_type=F32)

    @pl.when(f == pl.num_programs(1) - 1)
    def _():
        x = x1_scr[...] + gate2_ref[0] * acc_scr[...]
        ms = jnp.mean(x * x, axis=-1, keepdims=True)
        o_ref[...] = x * lax.rsqrt(ms + EPS) * gf_ref[...]


def _layer_tail(rec, att, gatt, x2, gate1, scale2, shift2, gate2, norm2_g, final_g, w_out_bf, w_up_bf, w_down_bf,
                tm, tf, mod_map):
    n, d = x2.shape
    dff = w_up_bf.shape[1]
    tok = pl.BlockSpec((tm, d), lambda i, f: (i, 0))
    mod_block = (1,) + gate1.shape[1:]
    mod_spec = pl.BlockSpec(mod_block, mod_map)
    vec = pl.BlockSpec((1, d), lambda i, f: (0, 0))
    return pl.pallas_call(
        _tail_kernel,
        grid=(n // tm, dff // tf),
        in_specs=[tok, tok, tok, tok, mod_spec, mod_spec, mod_spec, mod_spec, vec, vec,
                  pl.BlockSpec((d, d), lambda i, f: (0, 0), pipeline_mode=pl.Buffered(1)),
                  pl.BlockSpec((d, tf), lambda i, f: (0, f)),
                  pl.BlockSpec((tf, d), lambda i, f: (f, 0))],
        out_specs=tok,
        out_shape=jax.ShapeDtypeStruct((n, d), F32),
        scratch_shapes=[pltpu.VMEM((tm, d), F32), pltpu.VMEM((tm, d), BF16), pltpu.VMEM((tm, d), F32)],
        compiler_params=_cparams(("arbitrary", "arbitrary")),
        name="layer_tail",
    )(rec, att, gatt, x2, gate1, scale2, shift2, gate2, norm2_g.reshape(1, d), final_g.reshape(1, d),
      w_out_bf, w_up_bf, w_down_bf)


def _pick(n, pref):
    t = min(n, pref)
    assert n % t == 0
    return t


def kernel(x_prompt, x_sample, cache_k, cache_v, state_conv, state_lru, page_table, c_prompt, c_sample, rel_bias_table, norm1_g, norm2_g, w_mod, b_mod, w_in, conv_w, conv_b, lru_wa, lru_ba, lru_wx, lru_bx, lru_lambda, lambda_q1, lambda_k1, lambda_q2, lambda_k2, subln_g, w_out, w_up, w_down, final_norm_g):
    depth = w_in.shape[0]
    assert depth == 1
    layer = 0
    lam_init = 0.8 - 0.6 * math.exp(-0.3 * layer)
    bp, s, d = x_prompt.shape
    bs, t, _ = x_sample.shape
    n_pool, page = cache_k.shape[1], cache_k.shape[2]

    n_seq = bp + bs
    n_pad = -n_seq % (2 * SUBLANES)
    c_all = jnp.concatenate([c_prompt, c_sample, jnp.zeros((n_pad, d), F32)], axis=0)
    mod = _modulation(c_all, w_mod[layer], b_mod[layer])
    mods_p = [mod[:bp, i * d:(i + 1) * d].reshape(bp, 1, d) for i in range(6)]

    w_in_bf = w_in[layer].astype(BF16)
    w_kt_bf = w_in_bf[:, K_SEG * d:(K_SEG + 1) * d].T
    w_out_bf = w_out[layer].astype(BF16)
    w_up_bf = w_up[layer].astype(BF16)
    w_down_bf = w_down[layer].astype(BF16)
    wa_bf = lru_wa[layer].astype(BF16)
    wx_bf = lru_wx[layer].astype(BF16)
    lam = (jnp.exp(jnp.sum(lambda_q1[layer] * lambda_k1[layer]))
           - jnp.exp(jnp.sum(lambda_q2[layer] * lambda_k2[layer])) + lam_init).astype(F32)

    n = bp * s
    tm = _pick(s, 512)
    tps = s // tm
    shift1, scale1, gate1, shift2, scale2, gate2 = mods_p
    mod_map = lambda i, j: (i // tps, 0, 0)
    x2 = x_prompt.reshape(n, d)
    xrec, yrec, q, ktf, ktb, vf, vb, grec, gatt = _in_projection(
        x2, scale1, shift1, norm1_g[layer], w_in_bf, w_kt_bf, tm, mod_map, s, True)
    rec, conv_p, lru_p = _recurrent_branch(
        xrec.reshape(bp, s, d), yrec.reshape(bp, s, d), grec.reshape(bp, s, d),
        jnp.zeros((bp, CONV_WIDTH - 1, d), F32), jnp.zeros((bp, d), F32),
        conv_w[layer], conv_b[layer], wa_bf, lru_ba[layer], wx_bf, lru_bx[layer], lru_lambda[layer],
        1, _pick(s, 256), 0)
    k_prompt = jnp.transpose(ktf.reshape(1, bp, 2 * N_HEADS, HEAD_DIM, s), (0, 1, 4, 2, 3))
    v_prompt = vf.reshape(1, bp, s, N_HEADS, V_DIM)

    n_s = bs * t
    tm_s = _pick(n_s, 512)
    assert tm_s % t == 0
    past = page_table.shape[1] * page
    shift1_s, scale1_s, gate1_s, shift2_s, scale2_s, gate2_s = [
        jnp.repeat(mod[bp:bp + bs, i * d:(i + 1) * d], t, axis=0).reshape(n_s // tm_s, tm_s, d) for i in range(6)]
    mod_map_s = lambda i, j: (i, 0, 0)
    x2_s = x_sample.reshape(n_s, d)
    xrec_s, yrec_s, q_s, kf_s, vf_s, grec_s, gatt_s = _in_projection(
        x2_s, scale1_s, shift1_s, norm1_g[layer], w_in_bf, w_kt_bf, tm_s, mod_map_s, t, False)
    rec_s, conv_s, lru_s = _recurrent_branch(
        xrec_s.reshape(bs, t, d), yrec_s.reshape(bs, t, d), grec_s.reshape(bs, t, d), state_conv[layer],
        state_lru[layer], conv_w[layer], conv_b[layer], wa_bf, lru_ba[layer], wx_bf, lru_bx[layer],
        lru_lambda[layer], _pick(bs, 32), t, past)

    cache_kt = jnp.transpose(cache_k[layer], (0, 2, 3, 1)).reshape(n_pool, d, page)
    cache_v2 = cache_v[layer].reshape(n_pool, page * N_HEADS, V_DIM)
    att_s, att = _attention(q_s.reshape(bs, t, d), kf_s.reshape(bs, t, d), vf_s.reshape(bs, t, d), cache_kt, cache_v2,
                            page_table, q.reshape(bp, s, d), ktb, vb.reshape(bp, s, d),
                            rel_bias_table, lam, lam_init, subln_g[layer],
                            _pick(page_table.shape[1], 4), _pick(s, 512))

    tf = _pick(w_up_bf.shape[1], 1024)
    y_prompt = _layer_tail(rec.reshape(n, d), att.reshape(n, d), gatt, x2, gate1, scale2, shift2, gate2,
                           norm2_g[layer], final_norm_g, w_out_bf, w_up_bf, w_down_bf, tm, tf,
                           mod_map).reshape(bp, s, d)
    y_sample = _layer_tail(rec_s.reshape(n_s, d), att_s.reshape(n_s, d), gatt_s, x2_s, gate1_s, scale2_s, shift2_s,
                           gate2_s, norm2_g[layer], final_norm_g, w_out_bf, w_up_bf, w_down_bf, tm_s, tf,
                           mod_map_s).reshape(bs, t, d)
    k_sample = kf_s.reshape(1, bs, t, 2 * N_HEADS, HEAD_DIM)
    v_sample = vf_s.reshape(1, bs, t, N_HEADS, V_DIM)

    return (y_prompt, y_sample, k_prompt, v_prompt,
            conv_p.reshape(1, bp, CONV_WIDTH - 1, d), lru_p.reshape(1, bp, d),
            k_sample, v_sample,
            conv_s.reshape(1, bs, CONV_WIDTH - 1, d), lru_s.reshape(1, bs, d))
```

```python
import functools
import math

import numpy as np
import jax
import jax.numpy as jnp
from jax import lax
from jax.experimental import pallas as pl
from jax.experimental.pallas import tpu as pltpu

F32 = jnp.float32
BF16 = jnp.bfloat16

N_HEADS = 8
HEAD_DIM = 64
V_DIM = 2 * HEAD_DIM
LRU_BLOCKS = 4
CONV_WIDTH = 4
LRU_C = 8.0
NUM_BUCKETS = 32
MAX_DISTANCE = 128
EPS = 1e-6
NEG_INF = -1e30
N_SEG = 7
K_SEG = 3
LANES = 128
SUBLANES = 8
VMEM_LIMIT = 56 * 1024 * 1024
ROW_CHUNK = 256
MXU_TILE = 256
SAMPLE_RING_SLOTS = 8


def _lane_tile(x, n):
    return x if n == 1 else jnp.concatenate([x] * n, axis=1)


def _cparams(sem):
    return pltpu.CompilerParams(dimension_semantics=sem, vmem_limit_bytes=VMEM_LIMIT)


def _mod_kernel(c_ref, w_ref, b_ref, o_ref):
    c = c_ref[...]
    s = (c * jax.nn.sigmoid(c)).astype(BF16)
    o_ref[...] = jnp.dot(s, w_ref[...].astype(BF16), preferred_element_type=F32) + b_ref[...]


def _modulation(c_all, w_mod, b_mod):
    m, d = c_all.shape
    n = w_mod.shape[1]
    tn = d
    return pl.pallas_call(
        _mod_kernel,
        grid=(n // tn,),
        in_specs=[pl.BlockSpec((m, d), lambda j: (0, 0)),
                  pl.BlockSpec((d, tn), lambda j: (0, j)),
                  pl.BlockSpec((1, tn), lambda j: (0, j))],
        out_specs=pl.BlockSpec((m, tn), lambda j: (0, j)),
        out_shape=jax.ShapeDtypeStruct((m, n), F32),
        compiler_params=_cparams(("arbitrary",)),
        name="modulation",
    )(c_all, w_mod, b_mod.reshape(1, n))


def _inproj_kernel(x_ref, sc_ref, sh_ref, g_ref, w_ref, wkt_ref, *rest, keys_transposed):
    if keys_transposed:
        xrec_ref, yrec_ref, q_ref, kf_ref, kb_ref, vf_ref, vb_ref, grec_ref, gatt_ref, h_scr = rest
    else:
        xrec_ref, yrec_ref, q_ref, kf_ref, vf_ref, grec_ref, gatt_ref, h_scr = rest
    d = x_ref.shape[-1]

    x = x_ref[...]
    ms = jnp.mean(x * x, axis=-1, keepdims=True)
    y = x * lax.rsqrt(ms + EPS) * g_ref[...]
    h_scr[...] = (y * (1.0 + sc_ref[0]) + sh_ref[0]).astype(BF16)

    def project(j):
        return jnp.dot(h_scr[...], w_ref[:, j * d:(j + 1) * d], preferred_element_type=F32)

    xrec_ref[...] = project(0)
    yrec_ref[...] = project(1)
    q_ref[...] = (project(2) * (HEAD_DIM ** -0.5)).astype(BF16)
    if keys_transposed:
        zt = lax.dot_general(wkt_ref[...], h_scr[...], (((1,), (1,)), ((), ())), preferred_element_type=F32)
        kf_ref[0] = zt
        kb_ref[0] = zt.astype(BF16)
    else:
        kf_ref[...] = project(K_SEG)
    z = project(4)
    vf_ref[...] = z
    if keys_transposed:
        vb_ref[...] = z.astype(BF16)
    grec_ref[...] = project(5)
    gatt_ref[...] = project(6)


def _in_projection(x2, scale, shift, norm_g, w_in_bf, w_kt_bf, tm, mod_map, seq_len, keys_transposed):
    n, d = x2.shape
    mod_block = (1,) + scale.shape[1:]
    tok = lambda i, j: (i, 0)
    f32_out = jax.ShapeDtypeStruct((n, d), F32)
    bf_out = jax.ShapeDtypeStruct((n, d), BF16)
    out_spec = pl.BlockSpec((tm, d), tok)
    if keys_transposed:
        tps = seq_len // tm
        kt_spec = pl.BlockSpec((1, d, tm), lambda i, j: (i // tps, 0, i % tps))
        kt_f32 = jax.ShapeDtypeStruct((n // seq_len, d, seq_len), F32)
        kt_bf = jax.ShapeDtypeStruct((n // seq_len, d, seq_len), BF16)
        out_specs = [out_spec, out_spec, out_spec, kt_spec, kt_spec, out_spec, out_spec, out_spec, out_spec]
        out_shape = [f32_out, f32_out, bf_out, kt_f32, kt_bf, f32_out, bf_out, f32_out, f32_out]
    else:
        out_specs = [out_spec] * 7
        out_shape = [f32_out, f32_out, bf_out, f32_out, f32_out, f32_out, f32_out]
    return pl.pallas_call(
        functools.partial(_inproj_kernel, keys_transposed=keys_transposed),
        grid=(n // tm, 1),
        in_specs=[pl.BlockSpec((tm, d), tok),
                  pl.BlockSpec(mod_block, mod_map),
                  pl.BlockSpec(mod_block, mod_map),
                  pl.BlockSpec((1, d), lambda i, j: (0, 0)),
                  pl.BlockSpec((d, N_SEG * d), lambda i, j: (0, 0), pipeline_mode=pl.Buffered(1)),
                  pl.BlockSpec((d, d), lambda i, j: (0, 0), pipeline_mode=pl.Buffered(1))],
        out_specs=out_specs,
        out_shape=out_shape,
        scratch_shapes=[pltpu.VMEM((tm, d), BF16)],
        compiler_params=_cparams(("arbitrary", "arbitrary")),
        name="in_projection",
    )(x2, scale, shift, norm_g.reshape(1, d), w_in_bf, w_kt_bf)


def _rec_kernel(x_ref, y_ref, g_ref, cprev_ref, hprev_ref, cw_ref, cb_ref, wa_ref, ba_ref, wx_ref, bx_ref,
                lam_ref, o_ref, cnew_ref, hnew_ref, xpad, a_scr, u_scr, hcar, *, nb, t, pos0):
    d = x_ref.shape[-1]
    ti = pl.program_id(1)
    groups = t // SUBLANES
    r = nb * groups

    @pl.when(ti == 0)
    def _():
        xpad[:, SUBLANES - (CONV_WIDTH - 1):SUBLANES, :] = cprev_ref[...]
        hcar[...] = hprev_ref[...]

    @pl.when(ti > 0)
    def _():
        xpad[:, 0:SUBLANES, :] = xpad[:, t:t + SUBLANES, :]

    x = x_ref[...]
    xpad[:, SUBLANES:, :] = x
    xc = cb_ref[...] + cw_ref[CONV_WIDTH - 1:CONV_WIDTH, :] * x
    for jj in range(CONV_WIDTH - 1):
        off = SUBLANES - (CONV_WIDTH - 1) + jj
        xc = xc + cw_ref[jj:jj + 1, :] * xpad[:, off:off + t, :]
    cnew_ref[...] = xpad[:, t + SUBLANES - (CONV_WIDTH - 1):t + SUBLANES, :]

    xc2 = xc.reshape(nb * t, d)
    xcb = xc2.astype(BF16)
    bw = d // LRU_BLOCKS
    ga = jnp.concatenate([jnp.dot(xcb[:, n * bw:(n + 1) * bw], wa_ref[n], preferred_element_type=F32)
                          for n in range(LRU_BLOCKS)], axis=-1)
    gx = jnp.concatenate([jnp.dot(xcb[:, n * bw:(n + 1) * bw], wx_ref[n], preferred_element_type=F32)
                          for n in range(LRU_BLOCKS)], axis=-1)
    gate_r = jax.nn.sigmoid(ga + ba_ref[...])
    gate_i = jax.nn.sigmoid(gx + bx_ref[...])
    z = -lam_ref[...]
    softplus = jnp.maximum(z, 0.0) + jnp.log1p(jnp.exp(-jnp.abs(z)))
    log_a = (-LRU_C) * gate_r * softplus
    a = jnp.exp(log_a)
    mult = jnp.sqrt(jnp.tanh(-log_a) * (1.0 + a * a))
    pos = pos0 + ti * t + lax.broadcasted_iota(jnp.int32, (nb, t, d), 1).reshape(nb * t, d)
    mult = jnp.where(pos == 0, 1.0, mult)
    u = mult * gate_i * xc2

    a3 = a.reshape(r, SUBLANES, d)
    u3 = u.reshape(r, SUBLANES, d)
    row = lax.broadcasted_iota(jnp.int32, (r, SUBLANES, d), 1)
    for s in (1, 2, 4):
        a_sh = pltpu.roll(a3, s, axis=1)
        u_sh = pltpu.roll(u3, s, axis=1)
        ok = row >= s
        u3 = jnp.where(ok, a3 * u_sh + u3, u3)
        a3 = jnp.where(ok, a3 * a_sh, a3)

    if groups == 1:
        h3 = a3 * hcar[...] + u3
        hcar[...] = h3[:, SUBLANES - 1:SUBLANES, :]
        h2 = h3.reshape(nb * t, d)
    else:
        assert nb == 1
        a_scr[...] = a3
        u_scr[...] = u3

        def body(gi, hin):
            hg = a_scr[gi] * hin + u_scr[gi]
            u_scr[gi] = hg
            return hg[SUBLANES - 1:SUBLANES, :]

        hcar[0] = lax.fori_loop(0, groups, body, hcar[0])
        h2 = u_scr[...].reshape(nb * t, d)
    hnew_ref[...] = hcar[...]
    yv = y_ref[...].reshape(nb * t, d)
    gv = g_ref[...].reshape(nb * t, d)
    o_ref[...] = (jax.nn.sigmoid(gv) * (h2 * jax.nn.gelu(yv))).reshape(nb, t, d)


def _recurrent_branch(xrec, yrec, grec, conv_prev, h_prev, conv_w, conv_b, wa_bf, ba, wx_bf, bx, lam, nb, t, pos0):
    b, s, d = xrec.shape
    groups = t // SUBLANES
    r = nb * groups
    blk = pl.BlockSpec((nb, t, d), lambda bi, ti: (bi, ti, 0))
    vec = pl.BlockSpec((1, d), lambda bi, ti: (0, 0))
    wblk = pl.BlockSpec(wa_bf.shape, lambda bi, ti: (0, 0, 0))
    kern = functools.partial(_rec_kernel, nb=nb, t=t, pos0=pos0)
    return pl.pallas_call(
        kern,
        grid=(b // nb, s // t),
        in_specs=[blk, blk, blk,
                  pl.BlockSpec((nb, CONV_WIDTH - 1, d), lambda bi, ti: (bi, 0, 0)),
                  pl.BlockSpec((nb, 1, d), lambda bi, ti: (bi, 0, 0)),
                  pl.BlockSpec((CONV_WIDTH, d), lambda bi, ti: (0, 0)),
                  vec, wblk, vec, wblk, vec, vec],
        out_specs=[blk,
                   pl.BlockSpec((nb, CONV_WIDTH - 1, d), lambda bi, ti: (bi, 0, 0)),
                   pl.BlockSpec((nb, 1, d), lambda bi, ti: (bi, 0, 0))],
        out_shape=[jax.ShapeDtypeStruct((b, s, d), F32),
                   jax.ShapeDtypeStruct((b, CONV_WIDTH - 1, d), F32),
                   jax.ShapeDtypeStruct((b, 1, d), F32)],
        scratch_shapes=[pltpu.VMEM((nb, t + SUBLANES, d), F32),
                        pltpu.VMEM((r, SUBLANES, d), F32),
                        pltpu.VMEM((r, SUBLANES, d), F32),
                        pltpu.VMEM((nb, 1, d), F32)],
        compiler_params=_cparams(("arbitrary", "arbitrary")),
        name="recurrent_branch",
    )(xrec, yrec, grec, conv_prev, h_prev.reshape(b, 1, d), conv_w, conv_b.reshape(1, d), wa_bf,
      ba.reshape(1, d), wx_bf, bx.reshape(1, d), lam.reshape(1, d))


def _bucket_np(rel):
    n = np.maximum(rel, 0)
    max_exact = NUM_BUCKETS // 2
    ratio = np.log(np.maximum(n, max_exact).astype(np.float32) / np.float32(max_exact)) / np.float32(
        math.log(MAX_DISTANCE / max_exact))
    large = max_exact + (ratio * np.float32(NUM_BUCKETS - max_exact)).astype(np.int32)
    large = np.minimum(large, NUM_BUCKETS - 1)
    return np.where(n < max_exact, n, large).astype(np.int32)


def _far_distance():
    far = 1
    while not np.all(_bucket_np(np.arange(far, far + 4 * MAX_DISTANCE)) == NUM_BUCKETS - 1):
        far += 1
    return far


def _bias_of_distances(dist, table):
    vals = table[jnp.asarray(_bucket_np(dist))] - table[NUM_BUCKETS - 1]
    return jnp.where(jnp.asarray(dist >= 0)[:, None], vals, NEG_INF).T


def _toeplitz_bias(base, t, table):
    period = 2 * t
    idx = np.arange(period)
    col_minus_row = np.where(idx < t, idx, idx - period)
    dist = np.where(idx == t, -1, base - col_minus_row)
    vec = _bias_of_distances(dist, table)
    skew = jnp.tile(vec, (1, t))[:, :t * (period - 1)].reshape(-1, t, period - 1)
    return skew[:, :, :t]


def _query_rows_bias(offset, t, keys, table):
    u = np.arange(keys + t - 1)
    vec = _bias_of_distances(offset + (t - 1) - u, table)
    per_query = jnp.stack([vec[:, t - 1 - tq:t - 1 - tq + keys] for tq in range(t)], axis=1)
    return jnp.repeat(per_query, 2, axis=0).reshape(2 * N_HEADS * t, keys)


def _prompt_unit(kb, i, lam, q_ref, kt_ref, v_ref, bias_ref, g_ref, o_ref, vaug, m_scr, acc_scr, *, t, rc, lam_init):
    @pl.when((i == 0) & (kb == 0))
    def _():
        vaug[:, :V_DIM] = v_ref[0]
        vaug[:, V_DIM:] = jnp.ones((vaug.shape[0], V_DIM), BF16)

    @pl.when(kb == 0)
    def _():
        m_scr[...] = jnp.full(m_scr.shape, NEG_INF, F32)
        acc_scr[...] = jnp.zeros(acc_scr.shape, F32)

    q = q_ref[0]
    lane = lax.broadcasted_iota(jnp.int32, q.shape, 1)
    zero = jnp.zeros_like(q)
    qz = (jnp.where(lane < HEAD_DIM, q, zero), jnp.where(lane >= HEAD_DIM, q, zero))
    start = pl.multiple_of(kb * t, t)

    def add_bias(s, base, r0):
        tiles = []
        for c0 in range(0, s.shape[1], rc):
            tile = s[:, c0:c0 + rc]
            delta = base + r0 - c0
            if delta < 0:
                tile = jnp.full(tile.shape, NEG_INF, F32)
            elif delta < 2 * rc:
                tile = tile + bias_ref[0, delta // rc]
            tiles.append(tile)
        return jnp.concatenate(tiles, axis=1)

    def block(base):
        causal = base == 0
        kt = kt_ref[0, :, pl.ds(start, t)]
        va = vaug[pl.ds(start, t), :]
        chains = [(r0, mp) for r0 in range(0, t, rc) for mp in range(2)]

        def keys_of(r0):
            return min(t, -(-(r0 + rc) // MXU_TILE) * MXU_TILE) if causal else t

        def logits(chain):
            r0, mp = chain
            return jnp.dot(qz[mp][r0:r0 + rc], kt[:, :keys_of(r0)], preferred_element_type=F32)

        s_next = logits(chains[0])
        for n, (r0, mp) in enumerate(chains):
            rows = slice(r0, r0 + rc)
            nk = keys_of(r0)
            s = s_next
            if n + 1 < len(chains):
                s_next = logits(chains[n + 1])
            if base is not None:
                s = add_bias(s, base, r0)
            m_old = m_scr[mp, rows]
            m_new = jnp.maximum(m_old, jnp.max(s, axis=-1, keepdims=True))
            p = jnp.exp(s - _lane_tile(m_new, nk // LANES)).astype(BF16)
            alpha = jnp.exp(m_old - m_new)
            acc_scr[mp, rows] = (_lane_tile(alpha, 2) * acc_scr[mp, rows]
                                 + jnp.dot(p, va[:nk], preferred_element_type=F32))
            m_scr[mp, rows] = m_new

    @pl.when(kb < i - 1)
    def _():
        block(None)

    @pl.when(kb == i - 1)
    def _():
        block(t)

    @pl.when(kb == i)
    def _():
        block(0)
        o = (acc_scr[0, :, :V_DIM] / acc_scr[0, :, V_DIM:]
             - lam * (acc_scr[1, :, :V_DIM] / acc_scr[1, :, V_DIM:]))
        ms = jnp.mean(o * o, axis=-1, keepdims=True)
        o_ref[0] = o * lax.rsqrt(ms + EPS) * g_ref[...] * (1.0 - lam_init)


def _prompt_units(b, n_tiles, steps):
    units = np.array([(bi, h, i, kb) for bi in range(b) for h in range(N_HEADS)
                      for i in range(n_tiles) for kb in range(i + 1)], np.int32)
    n_units = len(units)
    assert n_units <= steps, "at most one prompt unit per grid step"
    done_before = (np.arange(steps + 1) * n_units) // steps
    has = (done_before[1:] > done_before[:-1]).astype(np.int32)
    per_step = units[np.minimum(done_before[:-1], n_units - 1)]
    return tuple(jnp.asarray(per_step[:, c]) for c in range(4)) + (jnp.asarray(has),)


def _attn_kernel(pt_ref, ub_ref, uh_ref, ui_ref, ukb_ref, uhas_ref, lam_ref, q_ref, knew_ref, vnew_ref, blast_ref,
                 bnew_ref, g_ref, ck_hbm, cv_hbm, pq_ref, pkt_ref, pv_ref, pbias_ref, o_ref, po_ref,
                 kbuf, vbuf, sem, qbd, knew_t, vpad, m_scr, l_scr, acc, p_scr, alpha_scr, vaug, pm_scr, pacc_scr,
                 *, gp, nbuf, lam_init, pt_tile, rc):
    del ub_ref, uh_ref
    gi = pl.program_id(1)
    n_groups = pl.num_programs(1)
    n_maps = 2 * N_HEADS
    t = q_ref.shape[1]
    d = q_ref.shape[2]
    rows = n_maps * t
    page = kbuf.shape[3]

    step = pl.program_id(0) * n_groups + gi
    total = pl.num_programs(0) * n_groups

    def page_copies(st, slot):
        sb = st // n_groups
        sg = st % n_groups
        out = []
        for jj in range(gp):
            pg = pt_ref[sb, sg * gp + jj]
            out.append(pltpu.make_async_copy(ck_hbm.at[pg], kbuf.at[slot, jj], sem.at[slot, 0, jj]))
            out.append(pltpu.make_async_copy(cv_hbm.at[pg], vbuf.at[slot, jj], sem.at[slot, 1, jj]))
        return out

    lookahead = nbuf - 2

    @pl.when(step == 0)
    def _():
        for st in range(lookahead):
            for c in page_copies(st, st):
                c.start()

    ahead = step + lookahead

    @pl.when(ahead < total)
    def _():
        for c in page_copies(ahead, ahead % nbuf):
            c.start()

    @pl.when(uhas_ref[step] == 1)
    def _():
        _prompt_unit(ukb_ref[step], ui_ref[step], lam_ref[0], pq_ref, pkt_ref, pv_ref, pbias_ref, g_ref, po_ref,
                     vaug, pm_scr, pacc_scr, t=pt_tile, rc=rc, lam_init=lam_init)

    slot = step % nbuf
    for c in page_copies(step, slot):
        c.wait()

    @pl.when(gi == 0)
    def _():
        qf = q_ref[0].astype(F32)
        qt = jnp.concatenate([qf] * n_maps, axis=0)
        rmap = lax.broadcasted_iota(jnp.int32, (rows, d), 0) // t
        cmap = lax.broadcasted_iota(jnp.int32, (rows, d), 1) // HEAD_DIM
        qbd[...] = jnp.where(rmap == cmap, qt, 0.0).astype(BF16)
        pad = jnp.zeros((page - t, d), F32)
        knew_t[...] = jnp.concatenate([knew_ref[0], pad], axis=0).T.astype(BF16)
        vpad[...] = jnp.concatenate([vnew_ref[0], pad], axis=0).astype(BF16)
        m_scr[...] = jnp.full(m_scr.shape, NEG_INF, F32)
        l_scr[...] = jnp.zeros(l_scr.shape, F32)
        acc[...] = jnp.zeros(acc.shape, F32)
        p_scr[...] = jnp.zeros(p_scr.shape, BF16)
        alpha_scr[...] = jnp.ones(alpha_scr.shape, F32)

    @pl.when(step == 0)
    def _():
        vbuf[nbuf - 1] = jnp.zeros(vbuf.shape[1:], F32)

    def logits(kt):
        return jnp.dot(qbd[...], kt, preferred_element_type=F32)

    def softmax_step(s, bias):
        if bias is not None:
            s = s + bias
        m_old = m_scr[...]
        m_new = jnp.maximum(m_old, jnp.max(s, axis=-1, keepdims=True))
        p = jnp.exp(s - _lane_tile(m_new, s.shape[1] // LANES))
        alpha = jnp.exp(m_old - m_new)
        l_scr[...] = alpha * l_scr[...] + jnp.sum(p, axis=-1, keepdims=True)
        m_scr[...] = m_new
        return p.astype(BF16), alpha

    def values_step(pb, alpha, v_heads):
        pv = jnp.concatenate([jnp.dot(pb[2 * t * h:2 * t * (h + 1), :], v_heads[h], preferred_element_type=F32)
                              for h in range(N_HEADS)], axis=0)
        acc[...] = alpha * acc[...] + pv

    def page_keys(sl):
        return jnp.concatenate([kbuf[sl, jj].astype(BF16) for jj in range(gp)], axis=1)

    def page_values(sl):
        return [jnp.concatenate([vbuf[sl, jj, pl.ds(h, page, stride=N_HEADS), :].astype(BF16) for jj in range(gp)],
                                axis=0)
                for h in range(N_HEADS)]

    def pipelined_step(bias):
        s = logits(page_keys(slot))
        values_step(p_scr[...], alpha_scr[...], page_values((step + nbuf - 1) % nbuf))
        return softmax_step(s, bias)

    last = pl.num_programs(1) - 1

    @pl.when(gi < last)
    def _():
        pb, alpha = pipelined_step(None)
        p_scr[...] = pb
        alpha_scr[...] = alpha

    @pl.when(gi == last)
    def _():
        pb, alpha = pipelined_step(blast_ref[...])
        values_step(pb, alpha, page_values(slot))
        pb, alpha = softmax_step(logits(knew_t[...]), bnew_ref[...])
        values_step(pb, alpha, [vpad[:, h * V_DIM:(h + 1) * V_DIM] for h in range(N_HEADS)])
        lam = lam_ref[0]
        on = acc[...] / l_scr[...]
        for h in range(N_HEADS):
            r0 = 2 * h * t
            o = on[r0:r0 + t, :] - lam * on[r0 + t:r0 + 2 * t, :]
            ms = jnp.mean(o * o, axis=-1, keepdims=True)
            o_ref[0, :, h * V_DIM:(h + 1) * V_DIM] = o * lax.rsqrt(ms + EPS) * g_ref[...] * (1.0 - lam_init)


def _attention(q, k_new, v_new, cache_kt, cache_v2, page_table, pq, pkt, pv, rel_table, lam, lam_init, subln_g,
               gp, pt_tile):
    b, t, d = q.shape
    bp, s_len, _ = pq.shape
    rc = min(pt_tile, ROW_CHUNK)
    assert 2 * rc + 1 >= _far_distance() + rc, "tiles two or more row chunks below the diagonal see a constant bias"
    pbias = jnp.stack([_toeplitz_bias(0, rc, rel_table), _toeplitz_bias(rc, rc, rel_table)], axis=1)
    page = cache_kt.shape[2]
    n_pages = page_table.shape[1]
    past = n_pages * page
    rows = 2 * N_HEADS * t
    assert rows == LANES and V_DIM == LANES and page == LANES and n_pages % gp == 0
    far = _far_distance()
    keys = gp * page
    assert past - ((n_pages - gp) * page - 1) >= far, "all page groups but the last must see a constant bias"
    bias_last = _query_rows_bias(keys, t, keys, rel_table)
    bias_new = _query_rows_bias(0, t, page, rel_table)

    n_groups = n_pages // gp
    unit_tables = _prompt_units(bp, s_len // pt_tile, b * n_groups)

    seq = lambda bi, gi, *_: (bi, 0, 0)
    fixed = lambda bi, gi, *_: (0, 0)

    def unit_map(pick):
        def index_map(bi, gi, pt, ub, uh, ui, ukb, uhas):
            step = bi * n_groups + gi
            return pick(ub[step], uh[step], ui[step])
        return index_map

    nbuf = SAMPLE_RING_SLOTS
    assert b * n_groups >= nbuf - 1
    kern = functools.partial(_attn_kernel, gp=gp, nbuf=nbuf, lam_init=lam_init, pt_tile=pt_tile, rc=rc)
    grid_spec = pltpu.PrefetchScalarGridSpec(
        num_scalar_prefetch=6,
        grid=(b, n_groups),
        in_specs=[pl.BlockSpec(memory_space=pltpu.SMEM),
                  pl.BlockSpec((1, t, d), seq),
                  pl.BlockSpec((1, t, d), seq),
                  pl.BlockSpec((1, t, d), seq),
                  pl.BlockSpec((rows, keys), fixed),
                  pl.BlockSpec((rows, page), fixed),
                  pl.BlockSpec((1, V_DIM), fixed),
                  pl.BlockSpec(memory_space=pl.ANY),
                  pl.BlockSpec(memory_space=pl.ANY),
                  pl.BlockSpec((1, pt_tile, V_DIM), unit_map(lambda ub, uh, ui: (ub, ui, uh))),
                  pl.BlockSpec((1, V_DIM, s_len), unit_map(lambda ub, uh, ui: (ub, uh, 0))),
                  pl.BlockSpec((1, s_len, V_DIM), unit_map(lambda ub, uh, ui: (ub, 0, uh))),
                  pl.BlockSpec((1, 2, rc, rc), unit_map(lambda ub, uh, ui: (uh, 0, 0, 0)))],
        out_specs=[pl.BlockSpec((1, t, d), seq),
                   pl.BlockSpec((1, pt_tile, V_DIM), unit_map(lambda ub, uh, ui: (ub, ui, uh)))],
        scratch_shapes=[pltpu.VMEM((nbuf, gp, d, page), F32),
                        pltpu.VMEM((nbuf, gp, page * N_HEADS, V_DIM), F32),
                        pltpu.SemaphoreType.DMA((nbuf, 2, gp)),
                        pltpu.VMEM((rows, d), BF16),
                        pltpu.VMEM((d, page), BF16),
                        pltpu.VMEM((page, d), BF16),
                        pltpu.VMEM((rows, LANES), F32),
                        pltpu.VMEM((rows, LANES), F32),
                        pltpu.VMEM((rows, V_DIM), F32),
                        pltpu.VMEM((rows, keys), BF16),
                        pltpu.VMEM((rows, LANES), F32),
                        pltpu.VMEM((s_len, 2 * V_DIM), BF16),
                        pltpu.VMEM((2, pt_tile, LANES), F32),
                        pltpu.VMEM((2, pt_tile, 2 * V_DIM), F32)])
    return pl.pallas_call(
        kern,
        grid_spec=grid_spec,
        out_shape=[jax.ShapeDtypeStruct((b, t, d), F32), jax.ShapeDtypeStruct((bp, s_len, d), F32)],
        compiler_params=_cparams(("arbitrary", "arbitrary")),
        name="attention",
    )(page_table, *unit_tables, lam.reshape(1), q, k_new, v_new, bias_last, bias_new, subln_g.reshape(1, V_DIM),
      cache_kt, cache_v2, pq, pkt, pv, pbias)


def _tail_kernel(rec_ref, att_ref, gatt_ref, x_ref, gate1_ref, sc_ref, sh_ref, gate2_ref, g2_ref, gf_ref,
                 wo_ref, wu_ref, wd_ref, o_ref, x1_scr, h_scr, acc_scr):
    f = pl.program_id(1)

    @pl.when(f == 0)
    def _():
        merged = rec_ref[...] + jax.nn.sigmoid(gatt_ref[...]) * att_ref[...]
        y = jnp.dot(merged.astype(BF16), wo_ref[...], preferred_element_type=F32)
        x = x_ref[...] + gate1_ref[0] * y
        x1_scr[...] = x
        ms = jnp.mean(x * x, axis=-1, keepdims=True)
        h = x * lax.rsqrt(ms + EPS) * g2_ref[...]
        h_scr[...] = (h * (1.0 + sc_ref[0]) + sh_ref[0]).astype(BF16)
        acc_scr[...] = jnp.zeros(acc_scr.shape, F32)

    up = jnp.dot(h_scr[...], wu_ref[...], preferred_element_type=F32)
    act = jnp.square(jnp.maximum(up, 0.0)).astype(BF16)
    acc_scr[...] += jnp.dot(act, wd_ref[...], preferred_element_type=F32)

    @pl.when(f == pl.num_programs(1) - 1)
    def _():
        x = x1_scr[...] + gate2_ref[0] * acc_scr[...]
        ms = jnp.mean(x * x, axis=-1, keepdims=True)
        o_ref[...] = x * lax.rsqrt(ms + EPS) * gf_ref[...]


def _layer_tail(rec, att, gatt, x2, gate1, scale2, shift2, gate2, norm2_g, final_g, w_out_bf, w_up_bf, w_down_bf,
                tm, tf, mod_map):
    n, d = x2.shape
    dff = w_up_bf.shape[1]
    tok = pl.BlockSpec((tm, d), lambda i, f: (i, 0))
    mod_block = (1,) + gate1.shape[1:]
    mod_spec = pl.BlockSpec(mod_block, mod_map)
    vec = pl.BlockSpec((1, d), lambda i, f: (0, 0))
    return pl.pallas_call(
        _tail_kernel,
        grid=(n // tm, dff // tf),
        in_specs=[tok, tok, tok, tok, mod_spec, mod_spec, mod_spec, mod_spec, vec, vec,
                  pl.BlockSpec((d, d), lambda i, f: (0, 0), pipeline_mode=pl.Buffered(1)),
                  pl.BlockSpec((d, tf), lambda i, f: (0, f)),
                  pl.BlockSpec((tf, d), lambda i, f: (f, 0))],
        out_specs=tok,
        out_shape=jax.ShapeDtypeStruct((n, d), F32),
        scratch_shapes=[pltpu.VMEM((tm, d), F32), pltpu.VMEM((tm, d), BF16), pltpu.VMEM((tm, d), F32)],
        compiler_params=_cparams(("arbitrary", "arbitrary")),
        name="layer_tail",
    )(rec, att, gatt, x2, gate1, scale2, shift2, gate2, norm2_g.reshape(1, d), final_g.reshape(1, d),
      w_out_bf, w_up_bf, w_down_bf)


def _pick(n, pref):
    t = min(n, pref)
    assert n % t == 0
    return t


def kernel(x_prompt, x_sample, cache_k, cache_v, state_conv, state_lru, page_table, c_prompt, c_sample, rel_bias_table, norm1_g, norm2_g, w_mod, b_mod, w_in, conv_w, conv_b, lru_wa, lru_ba, lru_wx, lru_bx, lru_lambda, lambda_q1, lambda_k1, lambda_q2, lambda_k2, subln_g, w_out, w_up, w_down, final_norm_g):
    depth = w_in.shape[0]
    assert depth == 1
    layer = 0
    lam_init = 0.8 - 0.6 * math.exp(-0.3 * layer)
    bp, s, d = x_prompt.shape
    bs, t, _ = x_sample.shape
    n_pool, page = cache_k.shape[1], cache_k.shape[2]

    n_seq = bp + bs
    n_pad = -n_seq % (2 * SUBLANES)
    c_all = jnp.concatenate([c_prompt, c_sample, jnp.zeros((n_pad, d), F32)], axis=0)
    mod = _modulation(c_all, w_mod[layer], b_mod[layer])
    mods_p = [mod[:bp, i * d:(i + 1) * d].reshape(bp, 1, d) for i in range(6)]

    w_in_bf = w_in[layer].astype(BF16)
    w_kt_bf = w_in[layer][:, K_SEG * d:(K_SEG + 1) * d].T.astype(BF16)
    w_out_bf = w_out[layer].astype(BF16)
    w_up_bf = w_up[layer].astype(BF16)
    w_down_bf = w_down[layer].astype(BF16)
    wa_bf = lru_wa[layer].astype(BF16)
    wx_bf = lru_wx[layer].astype(BF16)
    lam = (jnp.exp(jnp.sum(lambda_q1[layer] * lambda_k1[layer]))
           - jnp.exp(jnp.sum(lambda_q2[layer] * lambda_k2[layer])) + lam_init).astype(F32)

    n = bp * s
    tm = _pick(s, 512)
    tps = s // tm
    shift1, scale1, gate1, shift2, scale2, gate2 = mods_p
    mod_map = lambda i, j: (i // tps, 0, 0)
    x2 = x_prompt.reshape(n, d)
    xrec, yrec, q, ktf, ktb, vf, vb, grec, gatt = _in_projection(
        x2, scale1, shift1, norm1_g[layer], w_in_bf, w_kt_bf, tm, mod_map, s, True)
    rec, conv_p, lru_p = _recurrent_branch(
        xrec.reshape(bp, s, d), yrec.reshape(bp, s, d), grec.reshape(bp, s, d),
        jnp.zeros((bp, CONV_WIDTH - 1, d), F32), jnp.zeros((bp, d), F32),
        conv_w[layer], conv_b[layer], wa_bf, lru_ba[layer], wx_bf, lru_bx[layer], lru_lambda[layer],
        1, _pick(s, 256), 0)
    k_prompt = jnp.transpose(ktf.reshape(1, bp, 2 * N_HEADS, HEAD_DIM, s), (0, 1, 4, 2, 3))
    v_prompt = vf.reshape(1, bp, s, N_HEADS, V_DIM)

    n_s = bs * t
    tm_s = _pick(n_s, 512)
    assert tm_s % t == 0
    past = page_table.shape[1] * page
    shift1_s, scale1_s, gate1_s, shift2_s, scale2_s, gate2_s = [
        jnp.repeat(mod[bp:bp + bs, i * d:(i + 1) * d], t, axis=0).reshape(n_s // tm_s, tm_s, d) for i in range(6)]
    mod_map_s = lambda i, j: (i, 0, 0)
    x2_s = x_sample.reshape(n_s, d)
    xrec_s, yrec_s, q_s, kf_s, vf_s, grec_s, gatt_s = _in_projection(
        x2_s, scale1_s, shift1_s, norm1_g[layer], w_in_bf, w_kt_bf, tm_s, mod_map_s, t, False)
    rec_s, conv_s, lru_s = _recurrent_branch(
        xrec_s.reshape(bs, t, d), yrec_s.reshape(bs, t, d), grec_s.reshape(bs, t, d), state_conv[layer],
        state_lru[layer], conv_w[layer], conv_b[layer], wa_bf, lru_ba[layer], wx_bf, lru_bx[layer],
        lru_lambda[layer], _pick(bs, 32), t, past)

    cache_kt = jnp.transpose(cache_k[layer], (0, 2, 3, 1)).reshape(n_pool, d, page)
    cache_v2 = cache_v[layer].reshape(n_pool, page * N_HEADS, V_DIM)
    att_s, att = _attention(q_s.reshape(bs, t, d), kf_s.reshape(bs, t, d), vf_s.reshape(bs, t, d), cache_kt, cache_v2,
                            page_table, q.reshape(bp, s, d), ktb, vb.reshape(bp, s, d),
                            rel_bias_table, lam, lam_init, subln_g[layer],
                            _pick(page_table.shape[1], 4), _pick(s, 512))

    tf = _pick(w_up_bf.shape[1], 1024)
    y_prompt = _layer_tail(rec.reshape(n, d), att.reshape(n, d), gatt, x2, gate1, scale2, shift2, gate2,
                           norm2_g[layer], final_norm_g, w_out_bf, w_up_bf, w_down_bf, tm, tf,
                           mod_map).reshape(bp, s, d)
    y_sample = _layer_tail(rec_s.reshape(n_s, d), att_s.reshape(n_s, d), gatt_s, x2_s, gate1_s, scale2_s, shift2_s,
                           gate2_s, norm2_g[layer], final_norm_g, w_out_bf, w_up_bf, w_down_bf, tm_s, tf,
                           mod_map_s).reshape(bs, t, d)
    k_sample = kf_s.reshape(1, bs, t, 2 * N_HEADS, HEAD_DIM)
    v_sample = vf_s.reshape(1, bs, t, N_HEADS, V_DIM)

    return (y_prompt, y_sample, k_prompt, v_prompt,
            conv_p.reshape(1, bp, CONV_WIDTH - 1, d), lru_p.reshape(1, bp, d),
            k_sample, v_sample,
            conv_s.reshape(1, bs, CONV_WIDTH - 1, d), lru_s.reshape(1, bs, d))
```
